```python
import jax, jax.numpy as jnp
from jax import lax
import numpy as np

D_MODEL = 1024
BATCH = 16
SEQ = 4096
DEPTH = 2

GRID_W = 64
CTX_LEN = 256
N_MIXERS = 2
N_NA_LAYERS = (DEPTH + 1) // 2
N_HG_LAYERS = DEPTH // 2
BRANCH = D_MODEL
NA_HEADS = 16
NA_HEAD_DIM = BRANCH // NA_HEADS
NA_WIN_H = 8
NA_WIN_W = 16
NA_QCOL_BLOCK = 16
NA_BAND_W = NA_QCOL_BLOCK + NA_WIN_W - 1
HG_HEADS = 8
HG_HEAD_DIM = BRANCH // HG_HEADS
HG_CHUNK = 64
EPS = 1e-6

kernel_name = "hybrid_natten_hgrn2_diffusion_block"


def rms_norm(x, w):
    xf = x.astype(jnp.float32)
    y = xf * lax.rsqrt(jnp.mean(xf * xf, axis=-1, keepdims=True) + EPS)
    return (y * w.astype(jnp.float32)).astype(x.dtype)


def adaln_params(cond, w, b):
    m = jax.nn.silu(cond) @ w + b
    return jnp.split(m, 3, axis=-1)


def modulate(x, norm_w, shift, scale):
    return rms_norm(x, norm_w) * (1 + scale) + shift


def split_heads(t, n_heads):
    return t.reshape(t.shape[:-1] + (n_heads, t.shape[-1] // n_heads))


def _na_column_tables():
    cols = np.arange(GRID_W)
    col_start = np.clip(cols - NA_WIN_W // 2, 0, GRID_W - NA_WIN_W)
    n_cb = GRID_W // NA_QCOL_BLOCK
    band_start = np.minimum(col_start[np.arange(n_cb) * NA_QCOL_BLOCK], GRID_W - NA_BAND_W)
    band_cols = band_start[:, None] + np.arange(NA_BAND_W)[None, :]
    q_cols = cols.reshape(n_cb, NA_QCOL_BLOCK)
    q_start = col_start.reshape(n_cb, NA_QCOL_BLOCK)
    kcol = band_cols[:, None, :]
    in_window = (kcol >= q_start[..., None]) & (kcol < q_start[..., None] + NA_WIN_W)
    dc_index = np.clip(kcol - q_cols[..., None] + NA_WIN_W - 1, 0, 2 * NA_WIN_W - 2)
    return band_cols, in_window, dc_index


def na_attend(q, k, v, k_ctx, v_ctx, rpb):
    B, L, H, dh = q.shape
    rows = L // GRID_W
    kh = min(NA_WIN_H, rows)
    n_cb = GRID_W // NA_QCOL_BLOCK
    band_cols, in_window, dc_index = _na_column_tables()
    col_mask = jnp.asarray(in_window)[:, :, None, :]
    dc_idx = jnp.asarray(dc_index)[:, :, None, :]
    scale = dh ** -0.5
    qg = q.reshape(B, rows, n_cb, NA_QCOL_BLOCK, H, dh)
    kg = k.reshape(B, rows, GRID_W, H, dh)
    vg = v.reshape(B, rows, GRID_W, H, dh)
    n_loc = kh * NA_BAND_W

    def one_row(r):
        r0 = jnp.clip(r - kh // 2, 0, rows - kh)
        q_r = lax.dynamic_index_in_dim(qg, r, axis=1, keepdims=False)
        k_band = lax.dynamic_slice_in_dim(kg, r0, kh, axis=1)[:, :, band_cols]
        v_band = lax.dynamic_slice_in_dim(vg, r0, kh, axis=1)[:, :, band_cols]
        s_loc = jnp.einsum('bjqhd,bkjwhd->bhjqkw', q_r, k_band).astype(jnp.float32) * scale
        dr_idx = (r0 + jnp.arange(kh) - r + (NA_WIN_H - 1))[None, None, :, None]
        bias = rpb[:, dr_idx, dc_idx].astype(jnp.float32)
        s_loc = jnp.where(col_mask, s_loc + bias[None], -jnp.inf)
        s_loc = s_loc.reshape(B, H, n_cb, NA_QCOL_BLOCK, n_loc)
        s_ctx = jnp.einsum('bjqhd,bmhd->bhjqm', q_r, k_ctx).astype(jnp.float32) * scale
        p = jax.nn.softmax(jnp.concatenate([s_loc, s_ctx], axis=-1), axis=-1).astype(v.dtype)
        p_loc = p[..., :n_loc].reshape(B, H, n_cb, NA_QCOL_BLOCK, kh, NA_BAND_W)
        o = (jnp.einsum('bhjqkw,bkjwhd->bjqhd', p_loc, v_band)
             + jnp.einsum('bhjqm,bmhd->bjqhd', p[..., n_loc:], v_ctx))
        return o.reshape(B, GRID_W, H, dh)

    o = lax.map(one_row, jnp.arange(rows))
    return jnp.moveaxis(o, 0, 1).reshape(B, L, H, dh)


def dense_attention(q, k, v):
    s = jnp.einsum('bqhd,bkhd->bhqk', q, k).astype(jnp.float32) * (q.shape[-1] ** -0.5)
    p = jax.nn.softmax(s, axis=-1).astype(v.dtype)
    return jnp.einsum('bhqk,bkhd->bqhd', p, v)


def na_layer(x, xc, mod, mod_c, norm_w, w_in, rpb, w_out, need_ctx):
    shift, scale, gate = mod
    shift_c, scale_c, gate_c = mod_c
    B, L, _ = x.shape
    h = modulate(x, norm_w, shift, scale)
    q, k, v, z = jnp.split(h @ w_in, 4, axis=-1)
    hc = modulate(xc, norm_w, shift_c, scale_c)
    if need_ctx:
        qc, kc, vc, zc = jnp.split(hc @ w_in, 4, axis=-1)
    else:
        kc, vc = jnp.split(hc @ w_in[:, BRANCH:3 * BRANCH], 2, axis=-1)
    kc_h, vc_h = split_heads(kc, NA_HEADS), split_heads(vc, NA_HEADS)
    o = na_attend(split_heads(q, NA_HEADS), split_heads(k, NA_HEADS), split_heads(v, NA_HEADS), kc_h, vc_h, rpb)
    o = o.reshape(B, L, BRANCH) * jax.nn.silu(z)
    x = x + gate * (o @ w_out)
    if need_ctx:
        oc = dense_attention(split_heads(qc, NA_HEADS), kc_h, vc_h).reshape(xc.shape[:2] + (BRANCH,))
        xc = xc + gate_c * ((oc * jax.nn.silu(zc)) @ w_out)
    return x, xc


def hg_heads(t):
    return jnp.swapaxes(split_heads(t, HG_HEADS), 1, 2)


def hgrn_query(t):
    return hg_heads(jax.nn.silu(t.astype(jnp.float32)) * (HG_HEAD_DIM ** -0.5))


def hgrn_decay(raw, lb):
    raw = raw.astype(jnp.float32)
    log_f = jnp.logaddexp(jnp.log(lb), jnp.log1p(-lb) + jax.nn.log_sigmoid(raw))
    k = (1 - lb) * jax.nn.sigmoid(-raw)
    return hg_heads(log_f), hg_heads(k)


def hgrn_chunk_scan(q, k, v, log_f, s0):
    B, H, T, _ = q.shape
    n = T // HG_CHUNK
    causal = jnp.tril(jnp.ones((HG_CHUNK, HG_CHUNK), dtype=bool))[:, :, None]

    def to_chunks(a):
        return jnp.moveaxis(a.reshape(B, H, n, HG_CHUNK, a.shape[-1]), 2, 0)

    def step(S, inp):
        qc, kc, vc, ac = inp
        A = jnp.cumsum(ac, axis=2)
        o_inter = jnp.einsum('bhtk,bhkv->bhtv', qc * jnp.exp(A), S)
        rel = jnp.where(causal, A[:, :, :, None, :] - A[:, :, None, :, :], -jnp.inf)
        scores = jnp.einsum('bhtk,bhsk,bhtsk->bhts', qc, kc, jnp.exp(rel))
        o_intra = jnp.einsum('bhts,bhsv->bhtv', scores, vc)
        A_last = A[:, :, -1:, :]
        S_new = (jnp.exp(A_last[:, :, 0, :])[..., None] * S
                 + jnp.einsum('bhsk,bhsv->bhkv', kc * jnp.exp(A_last - A), vc))
        return S_new, o_inter + o_intra

    S_fin, o = lax.scan(step, s0, (to_chunks(q), to_chunks(k), to_chunks(v), to_chunks(log_f)))
    return jnp.moveaxis(o, 0, 2).reshape(B, H, T, v.shape[-1]), S_fin


def hgrn_final_state(k, v, log_f):
    A = jnp.cumsum(log_f, axis=2)
    return jnp.einsum('bhsk,bhsv->bhkv', k * jnp.exp(A[:, :, -1:, :] - A), v)


def hgrn_direction(q, q_c, v, v_c, raw_lat, raw_ctx, lb, reverse):
    log_f, k = hgrn_decay(raw_lat, lb)
    log_fc, k_c = hgrn_decay(raw_ctx, lb)
    order = (lambda t: jnp.flip(t, axis=2)) if reverse else (lambda t: t)
    if q_c is not None:
        s0 = jnp.zeros(k_c.shape[:2] + (HG_HEAD_DIM, HG_HEAD_DIM), jnp.float32)
        o_c, s_ctx = hgrn_chunk_scan(order(q_c), order(k_c), order(v_c), order(log_fc), s0)
        o_c = order(o_c)
    else:
        o_c = None
        s_ctx = hgrn_final_state(order(k_c), order(v_c), order(log_fc))
    o, _ = hgrn_chunk_scan(order(q), order(k), order(v), order(log_f), s_ctx)
    return order(o), o_c


def hgrn_readout(o, g, head_norm_w, w_out):
    o = rms_norm(jnp.swapaxes(o, 1, 2), head_norm_w)
    o = o.reshape(o.shape[:2] + (BRANCH,)).astype(g.dtype) * jax.nn.silu(g)
    return o @ w_out


def hgrn_layer(x, xc, mod, mod_c, norm_w, w_in, lb, head_norm_w, w_out, need_ctx):
    shift, scale, gate = mod
    shift_c, scale_c, gate_c = mod_c
    h = modulate(x, norm_w, shift, scale)
    q, i_lat, f_fwd, f_bwd, g = jnp.split(h @ w_in, 5, axis=-1)
    hc = modulate(xc, norm_w, shift_c, scale_c)
    if need_ctx:
        qc, i_ctx, fc_fwd, fc_bwd, gc = jnp.split(hc @ w_in, 5, axis=-1)
        q_c = hgrn_query(qc)
    else:
        i_ctx, fc_fwd, fc_bwd = jnp.split(hc @ w_in[:, BRANCH:4 * BRANCH], 3, axis=-1)
        q_c = None
    q_h = hgrn_query(q)
    v_h = hg_heads(i_lat.astype(jnp.float32))
    v_c = hg_heads(i_ctx.astype(jnp.float32))
    o_f, oc_f = hgrn_direction(q_h, q_c, v_h, v_c, f_fwd, fc_fwd, lb, False)
    o_b, oc_b = hgrn_direction(q_h, q_c, v_h, v_c, f_bwd, fc_bwd, lb, True)
    x = x + gate * hgrn_readout(o_f + o_b, g, head_norm_w, w_out)
    if need_ctx:
        xc = xc + gate_c * hgrn_readout(oc_f + oc_b, gc, head_norm_w, w_out)
    return x, xc


def setup_inputs(seed: int = 0) -> dict:
    key = jax.random.key(seed)
    ks = jax.random.split(key, 15)
    D, E = D_MODEL, BRANCH
    nrm = jax.random.normal
    return {
        "x": nrm(ks[0], (BATCH, SEQ, D), jnp.float32),
        "c": nrm(ks[1], (BATCH, D), jnp.float32),
        "ctx": nrm(ks[2], (BATCH, CTX_LEN, D), jnp.float32),
        "c_ctx": nrm(ks[3], (D,), jnp.float32),
        "ada_w": nrm(ks[4], (DEPTH, D, 3 * D), jnp.float32) * (0.5 * D ** -0.5),
        "ada_b": nrm(ks[5], (DEPTH, 3 * D), jnp.float32) * 0.02,
        "norm_w": 1.0 + 0.02 * nrm(ks[6], (DEPTH, D), jnp.float32),
        "na_w_in": nrm(ks[7], (N_NA_LAYERS, D, 4 * E), jnp.float32) * D ** -0.5,
        "na_rpb": nrm(ks[8], (N_NA_LAYERS, NA_HEADS, 2 * NA_WIN_H - 1, 2 * NA_WIN_W - 1), jnp.float32) * 0.5,
        "na_w_out": nrm(ks[9], (N_NA_LAYERS, E, D), jnp.float32) * E ** -0.5,
        "hg_w_in": nrm(ks[10], (N_HG_LAYERS, D, 5 * E), jnp.float32) * D ** -0.5,
        "hg_lower": nrm(ks[11], (DEPTH, E), jnp.float32) * 0.5,
        "hg_norm_w": 1.0 + 0.02 * nrm(ks[12], (N_HG_LAYERS, HG_HEAD_DIM), jnp.float32),
        "hg_w_out": nrm(ks[13], (N_HG_LAYERS, E, D), jnp.float32) * E ** -0.5,
        "final_norm_w": 1.0 + 0.02 * nrm(ks[14], (D,), jnp.float32),
    }


def reference(x, c, ctx, c_ctx, ada_w, ada_b, norm_w, na_w_in, na_rpb, na_w_out,
              hg_w_in, hg_lower, hg_norm_w, hg_w_out, final_norm_w):
    lb_all = jnp.cumsum(jax.nn.softmax(hg_lower.astype(jnp.float32), axis=0), axis=0)
    lb_all = lb_all - lb_all[0:1]
    xc = ctx
    for i in range(DEPTH):
        shift, scale, gate = adaln_params(c, ada_w[i], ada_b[i])
        mod = (shift[:, None, :], scale[:, None, :], gate[:, None, :])
        shift_c, scale_c, gate_c = adaln_params(c_ctx, ada_w[i], ada_b[i])
        mod_c = (shift_c, scale_c, gate_c)
        need_ctx = i < DEPTH - 1
        j = i // N_MIXERS
        if i % N_MIXERS == 0:
            x, xc = na_layer(x, xc, mod, mod_c, norm_w[i], na_w_in[j], na_rpb[j], na_w_out[j], need_ctx)
        else:
            x, xc = hgrn_layer(x, xc, mod, mod_c, norm_w[i], hg_w_in[j], lb_all[i], hg_norm_w[j], hg_w_out[j], need_ctx)
    return rms_norm(x, final_norm_w)
```

```python
import functools

import numpy as np
import jax
import jax.numpy as jnp
from jax import lax
from jax.experimental import pallas as pl
from jax.experimental.pallas import tpu as pltpu

D_MODEL = 1024
BRANCH = D_MODEL
GRID_W = 64
NA_HEADS = 16
NA_HEAD_DIM = BRANCH // NA_HEADS
NA_WIN_H = 8
NA_WIN_W = 16
HG_HEADS = 8
HG_HEAD_DIM = BRANCH // HG_HEADS
HG_CHUNK = 64
HG_DIAG = 8
EPS = 1e-6
LANES = 128
F32_TINY = 1e-37
VMEM_LIMIT = 56 * 1024 * 1024

BF16 = jnp.bfloat16
F32 = jnp.float32
NT_DIMS = (((1,), (1,)), ((), ()))
TN_DIMS = (((0,), (0,)), ((), ()))


def _dot(a, b):
    return jnp.dot(a, b, preferred_element_type=F32)


def _dot_nt(a, b):
    return lax.dot_general(a, b, NT_DIMS, preferred_element_type=F32)


def _dot_tn(a, b):
    return lax.dot_general(a, b, TN_DIMS, preferred_element_type=F32)


def _split3(x):
    hi = x.astype(BF16)
    r1 = x - hi.astype(F32)
    mid = r1.astype(BF16)
    lo = (r1 - mid.astype(F32)).astype(BF16)
    return hi, mid, lo


def _params(*sem):
    return pltpu.CompilerParams(dimension_semantics=sem, vmem_limit_bytes=VMEM_LIMIT)


def _adaln_kernel(cond_ref, w_ref, b_ref, out_ref):
    c = cond_ref[...]
    a = c * jax.nn.sigmoid(c)
    w = w_ref[0]
    a_hi = a.astype(BF16)
    a_lo = (a - a_hi.astype(F32)).astype(BF16)
    w_hi = w.astype(BF16)
    w_lo = (w - w_hi.astype(F32)).astype(BF16)
    acc = _dot(a_hi, w_hi) + _dot(a_hi, w_lo) + _dot(a_lo, w_hi)
    out_ref[0] = acc + b_ref[0]


def _adaln(cond, ada_w, ada_b):
    depth, d, n = ada_w.shape
    rows = cond.shape[0]
    tn = 512
    return pl.pallas_call(
        _adaln_kernel,
        grid=(depth, n // tn),
        in_specs=[
            pl.BlockSpec((rows, d), lambda l, j: (0, 0)),
            pl.BlockSpec((1, d, tn), lambda l, j: (l, 0, j)),
            pl.BlockSpec((1, 1, tn), lambda l, j: (l, 0, j)),
        ],
        out_specs=pl.BlockSpec((1, rows, tn), lambda l, j: (l, 0, j)),
        out_shape=jax.ShapeDtypeStruct((depth, rows, n), F32),
        compiler_params=_params("arbitrary", "arbitrary"),
        name="adaln",
    )(cond, ada_w, ada_b.reshape(depth, 1, n))


def _proj_kernel(x_ref, shift_ref, scale_ref, nw_ref, w_ref, *out_refs):
    x = x_ref[0]
    ms = jnp.mean(x * x, axis=-1, keepdims=True)
    y = x * lax.rsqrt(ms + EPS) * nw_ref[...]
    h = (y * (1.0 + scale_ref[0]) + shift_ref[0]).astype(BF16)
    tn = 512
    for g, o_ref in enumerate(out_refs):
        for j in range(BRANCH // tn):
            acc = _dot(h, w_ref[:, g * BRANCH + j * tn:g * BRANCH + (j + 1) * tn])
            o_ref[0, :, j * tn:(j + 1) * tn] = acc.astype(o_ref.dtype)


def _proj(x, shift, scale, norm_w, w_bf16, out_dtypes, name):
    bx, t, d = x.shape
    n = w_bf16.shape[1]
    tt = min(256, t)
    row_spec = pl.BlockSpec((1, tt, d), lambda b, i: (b, i, 0))
    mod_spec = pl.BlockSpec((1, 1, d), lambda b, i: (b, 0, 0))
    return pl.pallas_call(
        _proj_kernel,
        grid=(bx, t // tt),
        in_specs=[
            row_spec, mod_spec, mod_spec,
            pl.BlockSpec((1, d), lambda b, i: (0, 0)),
            pl.BlockSpec((d, n), lambda b, i: (0, 0)),
        ],
        out_specs=[pl.BlockSpec((1, tt, BRANCH), lambda b, i: (b, i, 0)) for _ in out_dtypes],
        out_shape=[jax.ShapeDtypeStruct((bx, t, BRANCH), dt) for dt in out_dtypes],
        compiler_params=_params("arbitrary", "arbitrary"),
        name=name,
    )(x, shift, scale, norm_w.reshape(1, d), w_bf16)


def _na_bias_table(rpb):
    cols = np.arange(GRID_W)
    col_start = np.clip(cols - NA_WIN_W // 2, 0, GRID_W - NA_WIN_W)
    kc = cols[None, :]
    qc = cols[:, None]
    in_window = (kc >= col_start[:, None]) & (kc < col_start[:, None] + NA_WIN_W)
    dc = np.clip(kc - qc + NA_WIN_W - 1, 0, 2 * NA_WIN_W - 2)
    t15 = jnp.where(jnp.asarray(in_window)[None, None], rpb[:, :, jnp.asarray(dc)], -jnp.inf)
    return jnp.concatenate([t15[:, :-1], t15[:, 1:]], axis=-1).astype(F32)


def _head_pair_select(o2, lane_lo):
    m = o2.shape[0] // 2
    return jnp.where(lane_lo, o2[:m], o2[m:])


def _na_kernel(q_ref, k_ref, v_ref, kc_ref, vc_ref, tp_ref, o_ref, *, rows):
    lane_lo = lax.broadcasted_iota(jnp.int32, (1, LANES), 1) < NA_HEAD_DIM
    kc = kc_ref[0]
    vc = vc_ref[0]
    kh = NA_WIN_H
    scale = NA_HEAD_DIM ** -0.5

    def row_step(r, carry):
        r0 = jnp.clip(r - kh // 2, 0, rows - kh)
        dr0 = r0 - r + (NA_WIN_H - 1)
        q = q_ref[0, pl.ds(pl.multiple_of(r * GRID_W, GRID_W), GRID_W), :] * scale
        zero = jnp.zeros_like(q)
        qq = jnp.concatenate([jnp.where(lane_lo, q, zero), jnp.where(lane_lo, zero, q)], axis=0)
        k0 = pl.multiple_of(r0 * GRID_W, GRID_W)
        kl = k_ref[0, pl.ds(k0, kh * GRID_W), :]
        vl = v_ref[0, pl.ds(k0, kh * GRID_W), :]
        s_loc = _dot_nt(qq, kl)
        bias = jnp.concatenate(
            [jnp.concatenate([tp_ref[0, dr0 + 2 * i], tp_ref[1, dr0 + 2 * i]], axis=0) for i in range(kh // 2)],
            axis=1)
        s_loc = s_loc + bias
        s_ctx = _dot_nt(qq, kc)
        m = jnp.maximum(jnp.max(s_loc, axis=-1, keepdims=True), jnp.max(s_ctx, axis=-1, keepdims=True))
        p_loc = jnp.exp(s_loc - m)
        p_ctx = jnp.exp(s_ctx - m)
        denom = jnp.sum(p_loc, axis=-1, keepdims=True) + jnp.sum(p_ctx, axis=-1, keepdims=True)
        o2 = (_dot(p_loc.astype(BF16), vl) + _dot(p_ctx.astype(BF16), vc)) / denom
        o_ref[0, pl.ds(pl.multiple_of(r * GRID_W, GRID_W), GRID_W), :] = _head_pair_select(o2, lane_lo).astype(o_ref.dtype)
        return carry

    lax.fori_loop(0, rows, row_step, 0)


def _na_attention(q, k, v, kc, vc, tp):
    b, l, e = q.shape
    ctx = kc.shape[1]
    rows = l // GRID_W
    assert rows >= NA_WIN_H and l % GRID_W == 0
    n_pairs = e // LANES
    lat = pl.BlockSpec((1, l, LANES), lambda hp, bi: (bi, 0, hp))
    cspec = pl.BlockSpec((1, ctx, LANES), lambda hp, bi: (bi, 0, hp))
    return pl.pallas_call(
        functools.partial(_na_kernel, rows=rows),
        grid=(n_pairs, b),
        in_specs=[lat, lat, lat, cspec, cspec,
                  pl.BlockSpec((2, 2 * NA_WIN_H - 2, GRID_W, LANES), lambda hp, bi: (hp, 0, 0, 0))],
        out_specs=lat,
        out_shape=jax.ShapeDtypeStruct((b, l, e), BF16),
        compiler_params=_params("arbitrary", "arbitrary"),
        name="na_attn",
    )(q, k, v, kc, vc, tp)


def _ctx_attn_kernel(q_ref, k_ref, v_ref, o_ref):
    lane_lo = lax.broadcasted_iota(jnp.int32, (1, LANES), 1) < NA_HEAD_DIM
    q = q_ref[0] * (NA_HEAD_DIM ** -0.5)
    zero = jnp.zeros_like(q)
    qq = jnp.concatenate([jnp.where(lane_lo, q, zero), jnp.where(lane_lo, zero, q)], axis=0)
    s = _dot_nt(qq, k_ref[0])
    m = jnp.max(s, axis=-1, keepdims=True)
    p = jnp.exp(s - m)
    o2 = _dot(p.astype(BF16), v_ref[0]) / jnp.sum(p, axis=-1, keepdims=True)
    o_ref[0] = _head_pair_select(o2, lane_lo).astype(o_ref.dtype)


def _ctx_attention(q, k, v):
    b, ctx, e = q.shape
    spec = pl.BlockSpec((1, ctx, LANES), lambda bi, hp: (bi, 0, hp))
    return pl.pallas_call(
        _ctx_attn_kernel,
        grid=(b, e // LANES),
        in_specs=[spec, spec, spec],
        out_specs=spec,
        out_shape=jax.ShapeDtypeStruct((b, ctx, e), BF16),
        compiler_params=_params("arbitrary", "arbitrary"),
        name="ctx_attn",
    )(q, k, v)


def _out_proj_kernel(x_ref, o_ref, z_ref, gate_ref, w_ref, fnw_ref, out_ref, *, final_norm):
    z = z_ref[0].astype(F32)
    u = (o_ref[0].astype(F32) * (z * jax.nn.sigmoid(z))).astype(BF16)
    xn = x_ref[0] + gate_ref[0] * _dot(u, w_ref[...])
    if final_norm:
        ms = jnp.mean(xn * xn, axis=-1, keepdims=True)
        xn = xn * lax.rsqrt(ms + EPS) * fnw_ref[...]
    out_ref[0] = xn


def _out_proj(x, o, z, gate, w_bf16, final_norm_w, final_norm, name):
    bx, t, d = x.shape
    tt = min(512, t)
    row = lambda width: pl.BlockSpec((1, tt, width), lambda b, i: (b, i, 0))
    return pl.pallas_call(
        functools.partial(_out_proj_kernel, final_norm=final_norm),
        grid=(bx, t // tt),
        in_specs=[
            row(d), row(BRANCH), row(BRANCH),
            pl.BlockSpec((1, 1, d), lambda b, i: (b, 0, 0)),
            pl.BlockSpec((BRANCH, d), lambda b, i: (0, 0)),
            pl.BlockSpec((1, d), lambda b, i: (0, 0)),
        ],
        out_specs=row(d),
        out_shape=jax.ShapeDtypeStruct((bx, t, d), F32),
        compiler_params=_params("arbitrary", "arbitrary"),
        name=name,
    )(x, o, z, gate, w_bf16, final_norm_w.reshape(1, d))


def _hgrn_kernel(q_ref, v_ref, ff_ref, fb_ref, vc_ref, ffc_ref, fbc_ref, lower_ref, hnw_ref, o_ref,
                 stf_ref, stb_ref, of_ref, ob_ref, *, n_lat, n_ctx, layer):
    c = HG_CHUNK
    dk = HG_HEAD_DIM

    lw = lower_ref[...]
    e = jnp.exp(lw - jnp.max(lw, axis=0, keepdims=True))
    sm = e / jnp.sum(e, axis=0, keepdims=True)
    lb = jnp.zeros((1, dk), F32)
    for j in range(1, layer + 1):
        lb = lb + sm[j:j + 1]
    one_m_lb = 1.0 - lb

    row = lax.broadcasted_iota(jnp.int32, (c, c), 0)
    col = lax.broadcasted_iota(jnp.int32, (c, c), 1)
    rowv = lax.broadcasted_iota(jnp.int32, (c, 1), 0)
    tri = {False: (col <= row).astype(BF16), True: (col >= row).astype(BF16)}

    def gates(raw):
        ex = jnp.exp(-jnp.abs(raw))
        r = 1.0 / (1.0 + ex)
        er = ex * r
        pos = raw >= 0
        f = lb + one_m_lb * jnp.where(pos, r, er)
        kk = one_m_lb * jnp.where(pos, er, r)
        return f, kk, jnp.log(jnp.maximum(f, F32_TINY))

    def chunk(qs, vb, raw, st_ref, rev, want_out):
        f, kk, logf = gates(raw)
        hi, mid, lo = _split3(logf)
        a = _dot(tri[rev], hi) + _dot(tri[rev], mid) + _dot(tri[rev], lo)
        last = a[0:1] if rev else a[c - 1:c]
        kst = (kk * jnp.exp(last - a)).astype(BF16)
        st = st_ref[...]
        out = None
        if want_out:
            o_inter = _dot_nt((qs * jnp.exp(a)).astype(BF16), st.astype(BF16))
            p = jnp.zeros((c, c), F32)
            half = c // 2
            while half >= HG_DIAG:
                pieces = []
                for blk in range(c // (2 * half)):
                    rr = blk * 2 * half + (half - 1 if rev else half)
                    pieces.append(jnp.broadcast_to(a[rr:rr + 1], (2 * half, dk)))
                aref = pieces[0] if len(pieces) == 1 else jnp.concatenate(pieces, axis=0)
                x = jnp.exp(-jnp.abs(a - aref))
                second = (rowv & (2 * half - 1)) >= half
                q_side = jnp.logical_not(second) if rev else second
                ql = jnp.where(q_side, qs * x, 0.0).astype(BF16)
                kl = jnp.where(q_side, 0.0, kk * x).astype(BF16)
                pl_ = _dot_nt(ql, kl)
                if 2 * half < c:
                    shift = int(np.log2(2 * half))
                    pl_ = jnp.where((row >> shift) == (col >> shift), pl_, 0.0)
                p = p + pl_
                half //= 2
            dd = None
            for dl in range(HG_DIAG):
                if dl == 0:
                    pr = qs * kk
                else:
                    fsh = f if dl == 1 else pltpu.roll(f, (c - (dl - 1)) if rev else (dl - 1), 0)
                    dd = fsh if dl == 1 else dd * fsh
                    pr = qs * pltpu.roll(kk, (c - dl) if rev else dl, 0) * dd
                rsum = jnp.sum(pr, axis=-1, keepdims=True)
                if rev:
                    msk = (col == row + dl) & ((row & (HG_DIAG - 1)) + dl < HG_DIAG)
                else:
                    msk = (col == row - dl) & ((row & (HG_DIAG - 1)) >= dl)
                p = p + jnp.where(msk, rsum, 0.0)
            out = o_inter + _dot(p.astype(BF16), vb)
        st_ref[...] = st * jnp.exp(last) + _dot_tn(vb, kst)
        return out

    stf_ref[...] = jnp.zeros_like(stf_ref)
    stb_ref[...] = jnp.zeros_like(stb_ref)

    def ctx_step(j, carry):
        sf = pl.multiple_of(j * c, c)
        sb = pl.multiple_of((n_ctx - 1 - j) * c, c)
        chunk(None, vc_ref[0, pl.ds(sf, c), :], ffc_ref[0, pl.ds(sf, c), :], stf_ref, False, False)
        chunk(None, vc_ref[0, pl.ds(sb, c), :], fbc_ref[0, pl.ds(sb, c), :], stb_ref, True, False)
        return carry

    lax.fori_loop(0, n_ctx, ctx_step, 0)

    qscale = HG_HEAD_DIM ** -0.5

    def query(sl):
        qr = q_ref[0, sl, :].astype(F32)
        return qr * jax.nn.sigmoid(qr) * qscale

    def lat_step(j, carry):
        sf = pl.multiple_of(j * c, c)
        sb = pl.multiple_of((n_lat - 1 - j) * c, c)
        slf = pl.ds(sf, c)
        slb = pl.ds(sb, c)
        of_ref[slf, :] = chunk(query(slf), v_ref[0, slf, :], ff_ref[0, slf, :], stf_ref, False, True)
        ob_ref[slb, :] = chunk(query(slb), v_ref[0, slb, :], fb_ref[0, slb, :], stb_ref, True, True)
        return carry

    lax.fori_loop(0, n_lat, lat_step, 0)

    o = of_ref[...] + ob_ref[...]
    ms = jnp.mean(o * o, axis=-1, keepdims=True)
    o_ref[0] = (o * lax.rsqrt(ms + EPS) * hnw_ref[...]).astype(o_ref.dtype)


def _hgrn_scan(q, v, ff, fb, vc, ffc, fbc, hg_lower, head_norm_w, layer):
    b, l, e = q.shape
    ctx = vc.shape[1]
    depth = hg_lower.shape[0]
    assert l % HG_CHUNK == 0 and ctx % HG_CHUNK == 0
    lat = pl.BlockSpec((1, l, LANES), lambda bi, h: (bi, 0, h))
    cspec = pl.BlockSpec((1, ctx, LANES), lambda bi, h: (bi, 0, h))
    return pl.pallas_call(
        functools.partial(_hgrn_kernel, n_lat=l // HG_CHUNK, n_ctx=ctx // HG_CHUNK, layer=layer),
        grid=(b, e // LANES),
        in_specs=[lat, lat, lat, lat, cspec, cspec, cspec,
                  pl.BlockSpec((depth, LANES), lambda bi, h: (0, h)),
                  pl.BlockSpec((1, LANES), lambda bi, h: (0, 0))],
        out_specs=lat,
        out_shape=jax.ShapeDtypeStruct((b, l, e), BF16),
        scratch_shapes=[
            pltpu.VMEM((HG_HEAD_DIM, HG_HEAD_DIM), F32),
            pltpu.VMEM((HG_HEAD_DIM, HG_HEAD_DIM), F32),
            pltpu.VMEM((l, LANES), F32),
            pltpu.VMEM((l, LANES), F32),
        ],
        compiler_params=_params("arbitrary", "arbitrary"),
        name="hgrn_scan",
    )(q, v, ff, fb, vc, ffc, fbc, hg_lower, head_norm_w.reshape(1, LANES))


def kernel(x, c, ctx, c_ctx, ada_w, ada_b, norm_w, na_w_in, na_rpb, na_w_out, hg_w_in, hg_lower, hg_norm_w,
           hg_w_out, final_norm_w):
    b, l, d = x.shape
    n_ctx_tok = ctx.shape[1]
    assert d == D_MODEL and HG_HEAD_DIM == LANES and 2 * NA_HEAD_DIM == LANES

    pad = (-(b + 1)) % 8
    cond = jnp.concatenate([c, c_ctx[None, :], jnp.zeros((pad, d), F32)], axis=0)
    mod = _adaln(cond, ada_w, ada_b)

    def mods(i):
        m = mod[i]
        lat = [m[:b, j * d:(j + 1) * d].reshape(b, 1, d) for j in range(3)]
        cx = [jnp.broadcast_to(m[b:b + 1, j * d:(j + 1) * d].reshape(1, 1, d), (b, 1, d)) for j in range(3)]
        return lat, cx

    (shift, scale, gate), (shift_c, scale_c, gate_c) = mods(0)
    w_in = na_w_in[0].astype(BF16)
    w_out = na_w_out[0].astype(BF16)
    q, k, v, z = _proj(x, shift, scale, norm_w[0], w_in, [BF16] * 4, "na_proj")
    qc, kc, vc, zc = _proj(ctx, shift_c, scale_c, norm_w[0], w_in, [BF16] * 4, "na_proj_ctx")
    o = _na_attention(q, k, v, kc, vc, _na_bias_table(na_rpb[0]))
    oc = _ctx_attention(qc, kc, vc)
    x1 = _out_proj(x, o, z, gate, w_out, final_norm_w, False, "na_out")
    xc1 = _out_proj(ctx, oc, zc, gate_c, w_out, final_norm_w, False, "na_out_ctx")

    (shift, scale, gate), (shift_c, scale_c, gate_c) = mods(1)
    w_in = hg_w_in[0].astype(BF16)
    w_out = hg_w_out[0].astype(BF16)
    dts = [BF16, BF16, F32, F32, BF16]
    q, i_lat, f_fwd, f_bwd, g = _proj(x1, shift, scale, norm_w[1], w_in, dts, "hg_proj")
    _, i_ctx, fc_fwd, fc_bwd, _ = _proj(xc1, shift_c, scale_c, norm_w[1], w_in, dts, "hg_proj_ctx")
    o = _hgrn_scan(q, i_lat, f_fwd, f_bwd, i_ctx, fc_fwd, fc_bwd, hg_lower, hg_norm_w[0], 1)
    return _out_proj(x1, o, g, gate, w_out, final_norm_w, True, "hg_out")
```

```python
import functools

import numpy as np
import jax
import jax.numpy as jnp
from jax import lax
from jax.experimental import pallas as pl
from jax.experimental.pallas import tpu as pltpu

D_MODEL = 1024
BRANCH = D_MODEL
GRID_W = 64
NA_HEADS = 16
NA_HEAD_DIM = BRANCH // NA_HEADS
NA_WIN_H = 8
NA_WIN_W = 16
HG_HEADS = 8
HG_HEAD_DIM = BRANCH // HG_HEADS
HG_CHUNK = 64
HG_DIAG = 8
HG_LEVELS = (32, 16, 8)
HG_UNROLL = 2
NA_UNROLL = 4
EPS = 1e-6
LANES = 128
F32_TINY = 1e-37
VMEM_LIMIT = 56 * 1024 * 1024

BF16 = jnp.bfloat16
F32 = jnp.float32
NT_DIMS = (((1,), (1,)), ((), ()))
TN_DIMS = (((0,), (0,)), ((), ()))


def _dot(a, b):
    return jnp.dot(a, b, preferred_element_type=F32)


def _dot_nt(a, b):
    return lax.dot_general(a, b, NT_DIMS, preferred_element_type=F32)


def _dot_tn(a, b):
    return lax.dot_general(a, b, TN_DIMS, preferred_element_type=F32)


def _split3(x):
    hi = x.astype(BF16)
    r1 = x - hi.astype(F32)
    mid = r1.astype(BF16)
    lo = (r1 - mid.astype(F32)).astype(BF16)
    return hi, mid, lo


def _params(*sem):
    return pltpu.CompilerParams(dimension_semantics=sem, vmem_limit_bytes=VMEM_LIMIT)


def _adaln_kernel(cond_ref, w_ref, b_ref, out_ref):
    c = cond_ref[...]
    a = c * jax.nn.sigmoid(c)
    w = w_ref[0]
    a_hi = a.astype(BF16)
    a_lo = (a - a_hi.astype(F32)).astype(BF16)
    w_hi = w.astype(BF16)
    w_lo = (w - w_hi.astype(F32)).astype(BF16)
    acc = _dot(a_hi, w_hi) + _dot(a_hi, w_lo) + _dot(a_lo, w_hi)
    out_ref[0] = acc + b_ref[0]


def _adaln(cond, ada_w, ada_b):
    depth, d, n = ada_w.shape
    rows = cond.shape[0]
    tn = 512
    return pl.pallas_call(
        _adaln_kernel,
        grid=(depth, n // tn),
        in_specs=[
            pl.BlockSpec((rows, d), lambda l, j: (0, 0)),
            pl.BlockSpec((1, d, tn), lambda l, j: (l, 0, j)),
            pl.BlockSpec((1, 1, tn), lambda l, j: (l, 0, j)),
        ],
        out_specs=pl.BlockSpec((1, rows, tn), lambda l, j: (l, 0, j)),
        out_shape=jax.ShapeDtypeStruct((depth, rows, n), F32),
        compiler_params=_params("arbitrary", "arbitrary"),
        name="adaln",
    )(cond, ada_w, ada_b.reshape(depth, 1, n))


def _proj_kernel(x_ref, shift_ref, scale_ref, nw_ref, w_ref, *out_refs):
    x = x_ref[0]
    ms = jnp.mean(x * x, axis=-1, keepdims=True)
    y = x * lax.rsqrt(ms + EPS) * nw_ref[...]
    h = (y * (1.0 + scale_ref[0]) + shift_ref[0]).astype(BF16)
    tn = 512
    for g, o_ref in enumerate(out_refs):
        for j in range(BRANCH // tn):
            acc = _dot(h, w_ref[:, g * BRANCH + j * tn:g * BRANCH + (j + 1) * tn])
            o_ref[0, :, j * tn:(j + 1) * tn] = acc.astype(o_ref.dtype)


def _proj(x, shift, scale, norm_w, w_bf16, out_dtypes, name):
    bx, t, d = x.shape
    n = w_bf16.shape[1]
    tt = min(256, t)
    row_spec = pl.BlockSpec((1, tt, d), lambda b, i: (b, i, 0))
    mod_spec = pl.BlockSpec((1, 1, d), lambda b, i: (b, 0, 0))
    return pl.pallas_call(
        _proj_kernel,
        grid=(bx, t // tt),
        in_specs=[
            row_spec, mod_spec, mod_spec,
            pl.BlockSpec((1, d), lambda b, i: (0, 0)),
            pl.BlockSpec((d, n), lambda b, i: (0, 0)),
        ],
        out_specs=[pl.BlockSpec((1, tt, BRANCH), lambda b, i: (b, i, 0)) for _ in out_dtypes],
        out_shape=[jax.ShapeDtypeStruct((bx, t, BRANCH), dt) for dt in out_dtypes],
        compiler_params=_params("arbitrary", "arbitrary"),
        name=name,
    )(x, shift, scale, norm_w.reshape(1, d), w_bf16)


def _na_bias_table(rpb):
    cols = np.arange(GRID_W)
    col_start = np.clip(cols - NA_WIN_W // 2, 0, GRID_W - NA_WIN_W)
    kc = cols[None, :]
    qc = cols[:, None]
    in_window = (kc >= col_start[:, None]) & (kc < col_start[:, None] + NA_WIN_W)
    dc = np.clip(kc - qc + NA_WIN_W - 1, 0, 2 * NA_WIN_W - 2)
    t15 = jnp.where(jnp.asarray(in_window)[None, None], rpb[:, :, jnp.asarray(dc)], -jnp.inf)
    return jnp.concatenate([t15[:, :-1], t15[:, 1:]], axis=-1).astype(F32)


def _head_pair_select(o2, lane_lo):
    m = o2.shape[0] // 2
    return jnp.where(lane_lo, o2[:m], o2[m:])


def _na_kernel(q_ref, k_ref, v_ref, kc_ref, vc_ref, tp_ref, o_ref, s_ref, o2_ref, inv_ref, *, rows):
    lane_lo = lax.broadcasted_iota(jnp.int32, (1, LANES), 1) < NA_HEAD_DIM
    kc = kc_ref[0]
    vc = vc_ref[0]
    kh = NA_WIN_H
    n_loc = kh * GRID_W
    scale = NA_HEAD_DIM ** -0.5

    def window(r):
        if isinstance(r, int):
            r0 = min(max(r - kh // 2, 0), rows - kh)
            return pl.ds(r0 * GRID_W, n_loc), r0 - r + (NA_WIN_H - 1)
        r0 = jnp.clip(r - kh // 2, 0, rows - kh)
        return pl.ds(pl.multiple_of(r0 * GRID_W, GRID_W), n_loc), r0 - r + (NA_WIN_H - 1)

    def query_rows(r):
        if isinstance(r, int):
            return pl.ds(r * GRID_W, GRID_W)
        return pl.ds(pl.multiple_of(r * GRID_W, GRID_W), GRID_W)

    def scores(r, slot):
        q = q_ref[0, query_rows(r), :] * scale
        zero = jnp.zeros_like(q)
        qq = jnp.concatenate([jnp.where(lane_lo, q, zero), jnp.where(lane_lo, zero, q)], axis=0)
        keys, _ = window(r)
        s_ref[slot, :, :n_loc] = _dot_nt(qq, k_ref[0, keys, :])
        s_ref[slot, :, n_loc:] = _dot_nt(qq, kc)

    def softmax_pv(r, slot):
        keys, dr0 = window(r)
        bias = jnp.concatenate(
            [jnp.concatenate([tp_ref[0, dr0 + 2 * i], tp_ref[1, dr0 + 2 * i]], axis=0) for i in range(kh // 2)],
            axis=1)
        s_loc = s_ref[slot, :, :n_loc] + bias
        s_ctx = s_ref[slot, :, n_loc:]
        m = jnp.maximum(jnp.max(s_loc, axis=-1, keepdims=True), jnp.max(s_ctx, axis=-1, keepdims=True))
        p_loc = jnp.exp(s_loc - m)
        p_ctx = jnp.exp(s_ctx - m)
        denom = jnp.sum(p_loc, axis=-1, keepdims=True) + jnp.sum(p_ctx, axis=-1, keepdims=True)
        o2_ref[slot] = _dot(p_loc.astype(BF16), v_ref[0, keys, :]) + _dot(p_ctx.astype(BF16), vc)
        inv_ref[slot] = 1.0 / denom

    def finish(r, slot):
        o2 = o2_ref[slot] * inv_ref[slot]
        o_ref[0, query_rows(r), :] = _head_pair_select(o2, lane_lo).astype(o_ref.dtype)

    u = NA_UNROLL
    for i in range(u):
        scores(i, i)
    for i in range(u):
        softmax_pv(i, i)
        scores(u + i, i)

    def steady(t, carry):
        r = t * u
        for i in range(u):
            finish(r - 2 * u + i, i)
            softmax_pv(r - u + i, i)
            scores(r + i, i)
        return carry

    lax.fori_loop(2, rows // u, steady, 0)
    for i in range(u):
        finish(rows - 2 * u + i, i)
        softmax_pv(rows - u + i, i)
    for i in range(u):
        finish(rows - u + i, i)


def _na_attention(q, k, v, kc, vc, tp):
    b, l, e = q.shape
    ctx = kc.shape[1]
    rows = l // GRID_W
    assert rows >= NA_WIN_H and l % GRID_W == 0
    assert rows % NA_UNROLL == 0 and rows // NA_UNROLL >= 2
    n_pairs = e // LANES
    lat = pl.BlockSpec((1, l, LANES), lambda hp, bi: (bi, 0, hp))
    cspec = pl.BlockSpec((1, ctx, LANES), lambda hp, bi: (bi, 0, hp))
    return pl.pallas_call(
        functools.partial(_na_kernel, rows=rows),
        grid=(n_pairs, b),
        in_specs=[lat, lat, lat, cspec, cspec,
                  pl.BlockSpec((2, 2 * NA_WIN_H - 2, GRID_W, LANES), lambda hp, bi: (hp, 0, 0, 0))],
        out_specs=lat,
        out_shape=jax.ShapeDtypeStruct((b, l, e), BF16),
        scratch_shapes=[
            pltpu.VMEM((NA_UNROLL, 2 * GRID_W, NA_WIN_H * GRID_W + ctx), F32),
            pltpu.VMEM((NA_UNROLL, 2 * GRID_W, LANES), F32),
            pltpu.VMEM((NA_UNROLL, 2 * GRID_W, 1), F32),
        ],
        compiler_params=_params("arbitrary", "arbitrary"),
        name="na_attn",
    )(q, k, v, kc, vc, tp)


def _ctx_attn_kernel(q_ref, k_ref, v_ref, o_ref):
    lane_lo = lax.broadcasted_iota(jnp.int32, (1, LANES), 1) < NA_HEAD_DIM
    q = q_ref[0] * (NA_HEAD_DIM ** -0.5)
    zero = jnp.zeros_like(q)
    qq = jnp.concatenate([jnp.where(lane_lo, q, zero), jnp.where(lane_lo, zero, q)], axis=0)
    s = _dot_nt(qq, k_ref[0])
    m = jnp.max(s, axis=-1, keepdims=True)
    p = jnp.exp(s - m)
    o2 = _dot(p.astype(BF16), v_ref[0]) / jnp.sum(p, axis=-1, keepdims=True)
    o_ref[0] = _head_pair_select(o2, lane_lo).astype(o_ref.dtype)


def _ctx_attention(q, k, v):
    b, ctx, e = q.shape
    spec = pl.BlockSpec((1, ctx, LANES), lambda bi, hp: (bi, 0, hp))
    return pl.pallas_call(
        _ctx_attn_kernel,
        grid=(b, e // LANES),
        in_specs=[spec, spec, spec],
        out_specs=spec,
        out_shape=jax.ShapeDtypeStruct((b, ctx, e), BF16),
        compiler_params=_params("arbitrary", "arbitrary"),
        name="ctx_attn",
    )(q, k, v)


def _out_proj_kernel(x_ref, o_ref, z_ref, gate_ref, w_ref, fnw_ref, out_ref, *, final_norm):
    z = z_ref[0].astype(F32)
    u = (o_ref[0].astype(F32) * (z * jax.nn.sigmoid(z))).astype(BF16)
    xn = x_ref[0] + gate_ref[0] * _dot(u, w_ref[...])
    if final_norm:
        ms = jnp.mean(xn * xn, axis=-1, keepdims=True)
        xn = xn * lax.rsqrt(ms + EPS) * fnw_ref[...]
    out_ref[0] = xn


def _out_proj(x, o, z, gate, w_bf16, final_norm_w, final_norm, name):
    bx, t, d = x.shape
    tt = min(512, t)
    row = lambda width: pl.BlockSpec((1, tt, width), lambda b, i: (b, i, 0))
    return pl.pallas_call(
        functools.partial(_out_proj_kernel, final_norm=final_norm),
        grid=(bx, t // tt),
        in_specs=[
            row(d), row(BRANCH), row(BRANCH),
            pl.BlockSpec((1, 1, d), lambda b, i: (b, 0, 0)),
            pl.BlockSpec((BRANCH, d), lambda b, i: (0, 0)),
            pl.BlockSpec((1, d), lambda b, i: (0, 0)),
        ],
        out_specs=row(d),
        out_shape=jax.ShapeDtypeStruct((bx, t, d), F32),
        compiler_params=_params("arbitrary", "arbitrary"),
        name=name,
    )(x, o, z, gate, w_bf16, final_norm_w.reshape(1, d))


def _hgrn_kernel(q_ref, v_ref, ff_ref, fb_ref, vc_ref, ffc_ref, fbc_ref, lower_ref, hnw_ref, o_ref,
                 st_ref, o_dir_ref, dmask_ref, lmask_ref, g_ref, p_ref, qk_ref, dl_ref, *, n_lat, n_ctx, layer):
    c = HG_CHUNK
    dk = HG_HEAD_DIM
    nblk = c // HG_DIAG

    lw = lower_ref[...]
    e = jnp.exp(lw - jnp.max(lw, axis=0, keepdims=True))
    sm = e / jnp.sum(e, axis=0, keepdims=True)
    lb = jnp.zeros((1, dk), F32)
    for j in range(1, layer + 1):
        lb = lb + sm[j:j + 1]
    one_m_lb = 1.0 - lb

    row = lax.broadcasted_iota(jnp.int32, (c, c), 0)
    col = lax.broadcasted_iota(jnp.int32, (c, c), 1)
    tri = {False: (col <= row).astype(BF16), True: (col >= row).astype(BF16)}

    for rev in (False, True):
        for dl in range(HG_DIAG):
            in_blk = (row & (HG_DIAG - 1)) + dl < HG_DIAG if rev else (row & (HG_DIAG - 1)) >= dl
            hit = col == (row + dl if rev else row - dl)
            dmask_ref[int(rev), dl] = (hit & in_blk).astype(F32)
        for li, half in enumerate(HG_LEVELS):
            shift = int(np.log2(2 * half))
            same = (row >> shift) == (col >> shift)
            r2 = (row & (2 * half - 1)) >= half
            c2 = (col & (2 * half - 1)) >= half
            pair = (jnp.logical_not(r2) & c2) if rev else (r2 & jnp.logical_not(c2))
            lmask_ref[int(rev), li] = (same & pair).astype(F32)

    def gates(raw):
        ex = jnp.exp(-jnp.abs(raw))
        r = 1.0 / (1.0 + ex)
        er = ex * r
        pos = raw >= 0
        f = lb + one_m_lb * jnp.where(pos, r, er)
        kk = one_m_lb * jnp.where(pos, er, r)
        return f, kk, jnp.log(jnp.maximum(f, F32_TINY))

    def decay_sums(raw, rev):
        f, kk, logf = gates(raw)
        hi, mid, lo = _split3(logf)
        a = _dot(tri[rev], hi) + _dot(tri[rev], mid) + _dot(tri[rev], lo)
        return f, kk, a

    def state_update(d, vb, kst, dlast):
        st = st_ref[d]
        st_ref[d] = st * dlast + _dot_tn(vb, kst)
        return st

    st_ref[...] = jnp.zeros_like(st_ref)

    for j in range(n_ctx):
        for d in (0, 1):
            rev = bool(d)
            sl = pl.ds((n_ctx - 1 - j if rev else j) * c, c)
            f, kk, a = decay_sums((fbc_ref if rev else ffc_ref)[0, sl, :], rev)
            last = a[0:1] if rev else a[c - 1:c]
            state_update(d, vc_ref[0, sl, :], (kk * jnp.exp(last - a)).astype(BF16), jnp.exp(last))

    qscale = HG_HEAD_DIM ** -0.5

    def rows_of(d, step):
        cidx = (n_lat - 1 - step) if d else step
        if isinstance(cidx, int):
            return pl.ds(cidx * c, c)
        return pl.ds(pl.multiple_of(cidx * c, c), c)

    def stage_sums(d, step):
        raw_ref = fb_ref if d else ff_ref
        f, kk, a = decay_sums(raw_ref[0, rows_of(d, step), :], bool(d))
        g_ref[d, 0] = f
        g_ref[d, 1] = kk
        g_ref[d, 2] = a

    def stage_scores(d, step):
        rev = bool(d)
        f = g_ref[d, 0]
        kk = g_ref[d, 1]
        a = g_ref[d, 2]
        qr = q_ref[0, rows_of(d, step), :].astype(F32)
        qs = qr * jax.nn.sigmoid(qr) * qscale
        last = a[0:1] if rev else a[c - 1:c]
        p = None
        for li, half in enumerate(HG_LEVELS):
            pieces = []
            for blk in range(c // (2 * half)):
                rr = blk * 2 * half + (half - 1 if rev else half)
                pieces.append(jnp.broadcast_to(a[rr:rr + 1], (2 * half, dk)))
            aref = pieces[0] if len(pieces) == 1 else jnp.concatenate(pieces, axis=0)
            x = jnp.exp(-jnp.abs(a - aref))
            pl_ = _dot_nt((qs * x).astype(BF16), (kk * x).astype(BF16)) * lmask_ref[d, li]
            p = pl_ if p is None else p + pl_
        f3 = f.reshape(nblk, HG_DIAG, dk)
        g3 = kk.reshape(nblk, HG_DIAG, dk)
        for dl in range(HG_DIAG):
            if dl > 0:
                g3 = f3 * pltpu.roll(g3, (HG_DIAG - 1) if rev else 1, 1)
            pr = qs * g3.reshape(c, dk)
            p = p + jnp.sum(pr, axis=-1, keepdims=True) * dmask_ref[d, dl]
        p_ref[d] = p.astype(BF16)
        qk_ref[d, 0] = (qs * jnp.exp(a)).astype(BF16)
        qk_ref[d, 1] = (kk * jnp.exp(last - a)).astype(BF16)
        dl_ref[d] = jnp.broadcast_to(jnp.exp(last), (8, dk))

    def stage_output(d, step):
        sl = rows_of(d, step)
        vb = v_ref[0, sl, :]
        pmat = p_ref[d]
        qa = qk_ref[d, 0]
        st = state_update(d, vb, qk_ref[d, 1], dl_ref[d, 0:1, :])
        o_dir_ref[d, sl, :] = _dot_nt(qa, st.astype(BF16)) + _dot(pmat, vb)

    def body(step_out, step_scores, step_sums):
        for d in (0, 1):
            if step_out is not None:
                stage_output(d, step_out)
            if step_scores is not None:
                stage_scores(d, step_scores)
            if step_sums is not None:
                stage_sums(d, step_sums)

    body(None, None, 0)
    body(None, 0, 1)

    def steady(j, carry):
        body(j - 2, j - 1, j)
        return carry

    lax.fori_loop(2, n_lat, steady, 0, unroll=HG_UNROLL)
    body(n_lat - 2, n_lat - 1, None)
    body(n_lat - 1, None, None)

    o = o_dir_ref[0] + o_dir_ref[1]
    ms = jnp.mean(o * o, axis=-1, keepdims=True)
    o_ref[0] = (o * lax.rsqrt(ms + EPS) * hnw_ref[...]).astype(o_ref.dtype)


def _hgrn_scan(q, v, ff, fb, vc, ffc, fbc, hg_lower, head_norm_w, layer):
    b, l, e = q.shape
    ctx = vc.shape[1]
    depth = hg_lower.shape[0]
    assert l % HG_CHUNK == 0 and ctx % HG_CHUNK == 0
    lat = pl.BlockSpec((1, l, LANES), lambda bi, h: (bi, 0, h))
    cspec = pl.BlockSpec((1, ctx, LANES), lambda bi, h: (bi, 0, h))
    return pl.pallas_call(
        functools.partial(_hgrn_kernel, n_lat=l // HG_CHUNK, n_ctx=ctx // HG_CHUNK, layer=layer),
        grid=(b, e // LANES),
        in_specs=[lat, lat, lat, lat, cspec, cspec, cspec,
                  pl.BlockSpec((depth, LANES), lambda bi, h: (0, h)),
                  pl.BlockSpec((1, LANES), lambda bi, h: (0, 0))],
        out_specs=lat,
        out_shape=jax.ShapeDtypeStruct((b, l, e), BF16),
        scratch_shapes=[
            pltpu.VMEM((2, HG_HEAD_DIM, HG_HEAD_DIM), F32),
            pltpu.VMEM((2, l, LANES), F32),
            pltpu.VMEM((2, HG_DIAG, HG_CHUNK, HG_CHUNK), F32),
            pltpu.VMEM((2, len(HG_LEVELS), HG_CHUNK, HG_CHUNK), F32),
            pltpu.VMEM((2, 3, HG_CHUNK, LANES), F32),
            pltpu.VMEM((2, HG_CHUNK, HG_CHUNK), BF16),
            pltpu.VMEM((2, 2, HG_CHUNK, LANES), BF16),
            pltpu.VMEM((2, 8, LANES), F32),
        ],
        compiler_params=_params("arbitrary", "arbitrary"),
        name="hgrn_scan",
    )(q, v, ff, fb, vc, ffc, fbc, hg_lower, head_norm_w.reshape(1, LANES))


def kernel(x, c, ctx, c_ctx, ada_w, ada_b, norm_w, na_w_in, na_rpb, na_w_out, hg_w_in, hg_lower, hg_norm_w,
           hg_w_out, final_norm_w):
    b, l, d = x.shape
    n_ctx_tok = ctx.shape[1]
    assert d == D_MODEL and HG_HEAD_DIM == LANES and 2 * NA_HEAD_DIM == LANES

    pad = (-(b + 1)) % 8
    cond = jnp.concatenate([c, c_ctx[None, :], jnp.zeros((pad, d), F32)], axis=0)
    mod = _adaln(cond, ada_w, ada_b)

    def mods(i):
        m = mod[i]
        lat = [m[:b, j * d:(j + 1) * d].reshape(b, 1, d) for j in range(3)]
        cx = [jnp.broadcast_to(m[b:b + 1, j * d:(j + 1) * d].reshape(1, 1, d), (b, 1, d)) for j in range(3)]
        return lat, cx

    (shift, scale, gate), (shift_c, scale_c, gate_c) = mods(0)
    w_in = na_w_in[0].astype(BF16)
    w_out = na_w_out[0].astype(BF16)
    q, k, v, z = _proj(x, shift, scale, norm_w[0], w_in, [BF16] * 4, "na_proj")
    qc, kc, vc, zc = _proj(ctx, shift_c, scale_c, norm_w[0], w_in, [BF16] * 4, "na_proj_ctx")
    o = _na_attention(q, k, v, kc, vc, _na_bias_table(na_rpb[0]))
    oc = _ctx_attention(qc, kc, vc)
    x1 = _out_proj(x, o, z, gate, w_out, final_norm_w, False, "na_out")
    xc1 = _out_proj(ctx, oc, zc, gate_c, w_out, final_norm_w, False, "na_out_ctx")

    (shift, scale, gate), (shift_c, scale_c, gate_c) = mods(1)
    w_in = hg_w_in[0].astype(BF16)
    w_out = hg_w_out[0].astype(BF16)
    dts = [BF16, BF16, F32, F32, BF16]
    q, i_lat, f_fwd, f_bwd, g = _proj(x1, shift, scale, norm_w[1], w_in, dts, "hg_proj")
    _, i_ctx, fc_fwd, fc_bwd, _ = _proj(xc1, shift_c, scale_c, norm_w[1], w_in, dts, "hg_proj_ctx")
    o = _hgrn_scan(q, i_lat, f_fwd, f_bwd, i_ctx, fc_fwd, fc_bwd, hg_lower, hg_norm_w[0], 1)
    return _out_proj(x1, o, g, gate, w_out, final_norm_w, True, "hg_out")
```

```python
import functools

import numpy as np
import jax
import jax.numpy as jnp
from jax import lax
from jax.experimental import pallas as pl
from jax.experimental.pallas import tpu as pltpu

D_MODEL = 1024
BRANCH = D_MODEL
GRID_W = 64
NA_HEADS = 16
NA_HEAD_DIM = BRANCH // NA_HEADS
NA_WIN_H = 8
NA_WIN_W = 16
HG_HEADS = 8
HG_HEAD_DIM = BRANCH // HG_HEADS
HG_CHUNK = 64
HG_DIAG = 4
HG_LEVELS = (32, 16, 8, 4)
HG_UNROLL = 4
NA_UNROLL = 4
EPS = 1e-6
LANES = 128
SUBLANES = 8
F32_TINY = 1e-37
VMEM_LIMIT = 56 * 1024 * 1024
PROJ_TN = 512

BF16 = jnp.bfloat16
F32 = jnp.float32
NT_DIMS = (((1,), (1,)), ((), ()))
TN_DIMS = (((0,), (0,)), ((), ()))


def _dot(a, b):
    return jnp.dot(a, b, preferred_element_type=F32)


def _dot_nt(a, b):
    return lax.dot_general(a, b, NT_DIMS, preferred_element_type=F32)


def _dot_tn(a, b):
    return lax.dot_general(a, b, TN_DIMS, preferred_element_type=F32)


def _split3(x):
    hi = x.astype(BF16)
    r1 = x - hi.astype(F32)
    mid = r1.astype(BF16)
    lo = (r1 - mid.astype(F32)).astype(BF16)
    return hi, mid, lo


def _params(*sem):
    return pltpu.CompilerParams(dimension_semantics=sem, vmem_limit_bytes=VMEM_LIMIT)


def _adaln_kernel(cond_ref, w_ref, b_ref, out_ref):
    c = cond_ref[...]
    a = c * jax.nn.sigmoid(c)
    w = w_ref[0]
    a_hi = a.astype(BF16)
    a_lo = (a - a_hi.astype(F32)).astype(BF16)
    w_hi = w.astype(BF16)
    w_lo = (w - w_hi.astype(F32)).astype(BF16)
    acc = _dot(a_hi, w_hi) + _dot(a_hi, w_lo) + _dot(a_lo, w_hi)
    out_ref[0] = acc + b_ref[0]


def _adaln(cond, ada_w, ada_b):
    depth, d, n = ada_w.shape
    rows = cond.shape[0]
    tn = 512
    return pl.pallas_call(
        _adaln_kernel,
        grid=(depth, n // tn),
        in_specs=[
            pl.BlockSpec((rows, d), lambda l, j: (0, 0)),
            pl.BlockSpec((1, d, tn), lambda l, j: (l, 0, j)),
            pl.BlockSpec((1, 1, tn), lambda l, j: (l, 0, j)),
        ],
        out_specs=pl.BlockSpec((1, rows, tn), lambda l, j: (l, 0, j)),
        out_shape=jax.ShapeDtypeStruct((depth, rows, n), F32),
        compiler_params=_params("arbitrary", "arbitrary"),
        name="adaln",
    )(cond, ada_w, ada_b.reshape(depth, 1, n))


def _modulated_norm(x, nw_ref, shift_ref, scale_ref):
    ms = jnp.mean(x * x, axis=-1, keepdims=True)
    y = x * lax.rsqrt(ms + EPS) * nw_ref[...]
    return (y * (1.0 + scale_ref[0]) + shift_ref[0]).astype(BF16)


def _project(h, w_ref, out_refs):
    for g, o_ref in enumerate(out_refs):
        for j in range(BRANCH // PROJ_TN):
            c0 = g * BRANCH + j * PROJ_TN
            o_ref[0, :, j * PROJ_TN:(j + 1) * PROJ_TN] = _dot(h, w_ref[:, c0:c0 + PROJ_TN]).astype(o_ref.dtype)


def _proj_kernel(x_ref, shift_ref, scale_ref, nw_ref, w_ref, *out_refs):
    _project(_modulated_norm(x_ref[0], nw_ref, shift_ref, scale_ref), w_ref, out_refs)


def _proj(x, shift, scale, norm_w, w_bf16, out_dtypes, name):
    bx, t, d = x.shape
    n = w_bf16.shape[1]
    tt = min(256, t)
    row_spec = pl.BlockSpec((1, tt, d), lambda b, i: (b, i, 0))
    mod_spec = pl.BlockSpec((1, 1, d), lambda b, i: (b, 0, 0))
    return pl.pallas_call(
        _proj_kernel,
        grid=(bx, t // tt),
        in_specs=[
            row_spec, mod_spec, mod_spec,
            pl.BlockSpec((1, d), lambda b, i: (0, 0)),
            pl.BlockSpec((d, n), lambda b, i: (0, 0)),
        ],
        out_specs=[pl.BlockSpec((1, tt, BRANCH), lambda b, i: (b, i, 0)) for _ in out_dtypes],
        out_shape=[jax.ShapeDtypeStruct((bx, t, BRANCH), dt) for dt in out_dtypes],
        compiler_params=_params("arbitrary", "arbitrary"),
        name=name,
    )(x, shift, scale, norm_w.reshape(1, d), w_bf16)


def _na_bias_table(rpb):
    cols = np.arange(GRID_W)
    col_start = np.clip(cols - NA_WIN_W // 2, 0, GRID_W - NA_WIN_W)
    kc = cols[None, :]
    qc = cols[:, None]
    in_window = (kc >= col_start[:, None]) & (kc < col_start[:, None] + NA_WIN_W)
    dc = np.clip(kc - qc + NA_WIN_W - 1, 0, 2 * NA_WIN_W - 2)
    t15 = jnp.where(jnp.asarray(in_window)[None, None], rpb[:, :, jnp.asarray(dc)], -jnp.inf)
    return jnp.concatenate([t15[:, :-1], t15[:, 1:]], axis=-1).astype(F32)


def _head_pair_select(o2, lane_lo):
    m = o2.shape[0] // 2
    return jnp.where(lane_lo, o2[:m], o2[m:])


def _na_kernel(q_ref, k_ref, v_ref, kc_ref, vc_ref, tp_ref, o_ref, s_ref, o2_ref, inv_ref, *, rows):
    lane_lo = lax.broadcasted_iota(jnp.int32, (1, LANES), 1) < NA_HEAD_DIM
    kc = kc_ref[0]
    vc = vc_ref[0]
    kh = NA_WIN_H
    n_loc = kh * GRID_W
    scale = NA_HEAD_DIM ** -0.5

    def window(r):
        if isinstance(r, int):
            r0 = min(max(r - kh // 2, 0), rows - kh)
            return pl.ds(r0 * GRID_W, n_loc), r0 - r + (NA_WIN_H - 1)
        r0 = jnp.clip(r - kh // 2, 0, rows - kh)
        return pl.ds(pl.multiple_of(r0 * GRID_W, GRID_W), n_loc), r0 - r + (NA_WIN_H - 1)

    def query_rows(r):
        if isinstance(r, int):
            return pl.ds(r * GRID_W, GRID_W)
        return pl.ds(pl.multiple_of(r * GRID_W, GRID_W), GRID_W)

    def scores(r, slot):
        q = q_ref[0, query_rows(r), :] * scale
        zero = jnp.zeros_like(q)
        qq = jnp.concatenate([jnp.where(lane_lo, q, zero), jnp.where(lane_lo, zero, q)], axis=0)
        keys, _ = window(r)
        s_ref[slot, :, :n_loc] = _dot_nt(qq, k_ref[0, keys, :])
        s_ref[slot, :, n_loc:] = _dot_nt(qq, kc)

    def softmax_pv(r, slot):
        keys, dr0 = window(r)
        bias = jnp.concatenate(
            [jnp.concatenate([tp_ref[0, dr0 + 2 * i], tp_ref[1, dr0 + 2 * i]], axis=0) for i in range(kh // 2)],
            axis=1)
        s_loc = s_ref[slot, :, :n_loc] + bias
        s_ctx = s_ref[slot, :, n_loc:]
        m = jnp.maximum(jnp.max(s_loc, axis=-1, keepdims=True), jnp.max(s_ctx, axis=-1, keepdims=True))
        p_loc = jnp.exp(s_loc - m)
        p_ctx = jnp.exp(s_ctx - m)
        denom = jnp.sum(p_loc, axis=-1, keepdims=True) + jnp.sum(p_ctx, axis=-1, keepdims=True)
        o2_ref[slot] = _dot(p_loc.astype(BF16), v_ref[0, keys, :]) + _dot(p_ctx.astype(BF16), vc)
        inv_ref[slot] = 1.0 / denom

    def finish(r, slot):
        o2 = o2_ref[slot] * inv_ref[slot]
        o_ref[0, query_rows(r), :] = _head_pair_select(o2, lane_lo).astype(o_ref.dtype)

    u = NA_UNROLL
    for i in range(u):
        scores(i, i)
    for i in range(u):
        softmax_pv(i, i)
        scores(u + i, i)

    def steady(t, carry):
        r = t * u
        for i in range(u):
            finish(r - 2 * u + i, i)
            softmax_pv(r - u + i, i)
            scores(r + i, i)
        return carry

    lax.fori_loop(2, rows // u, steady, 0)
    for i in range(u):
        finish(rows - 2 * u + i, i)
        softmax_pv(rows - u + i, i)
    for i in range(u):
        finish(rows - u + i, i)


def _na_attention(q, k, v, kc, vc, tp):
    b, l, e = q.shape
    ctx = kc.shape[1]
    rows = l // GRID_W
    assert rows >= NA_WIN_H and l % GRID_W == 0
    assert rows % NA_UNROLL == 0 and rows // NA_UNROLL >= 2
    n_pairs = e // LANES
    lat = pl.BlockSpec((1, l, LANES), lambda hp, bi: (bi, 0, hp))
    cspec = pl.BlockSpec((1, ctx, LANES), lambda hp, bi: (bi, 0, hp))
    return pl.pallas_call(
        functools.partial(_na_kernel, rows=rows),
        grid=(n_pairs, b),
        in_specs=[lat, lat, lat, cspec, cspec,
                  pl.BlockSpec((2, 2 * NA_WIN_H - 2, GRID_W, LANES), lambda hp, bi: (hp, 0, 0, 0))],
        out_specs=lat,
        out_shape=jax.ShapeDtypeStruct((b, l, e), BF16),
        scratch_shapes=[
            pltpu.VMEM((NA_UNROLL, 2 * GRID_W, NA_WIN_H * GRID_W + ctx), F32),
            pltpu.VMEM((NA_UNROLL, 2 * GRID_W, LANES), F32),
            pltpu.VMEM((NA_UNROLL, 2 * GRID_W, 1), F32),
        ],
        compiler_params=_params("arbitrary", "arbitrary"),
        name="na_attn",
    )(q, k, v, kc, vc, tp)


def _ctx_attn_kernel(q_ref, k_ref, v_ref, o_ref):
    lane_lo = lax.broadcasted_iota(jnp.int32, (1, LANES), 1) < NA_HEAD_DIM
    q = q_ref[0] * (NA_HEAD_DIM ** -0.5)
    zero = jnp.zeros_like(q)
    qq = jnp.concatenate([jnp.where(lane_lo, q, zero), jnp.where(lane_lo, zero, q)], axis=0)
    s = _dot_nt(qq, k_ref[0])
    m = jnp.max(s, axis=-1, keepdims=True)
    p = jnp.exp(s - m)
    o2 = _dot(p.astype(BF16), v_ref[0]) / jnp.sum(p, axis=-1, keepdims=True)
    o_ref[0] = _head_pair_select(o2, lane_lo).astype(o_ref.dtype)


def _ctx_attention(q, k, v):
    b, ctx, e = q.shape
    spec = pl.BlockSpec((1, ctx, LANES), lambda bi, hp: (bi, 0, hp))
    return pl.pallas_call(
        _ctx_attn_kernel,
        grid=(b, e // LANES),
        in_specs=[spec, spec, spec],
        out_specs=spec,
        out_shape=jax.ShapeDtypeStruct((b, ctx, e), BF16),
        compiler_params=_params("arbitrary", "arbitrary"),
        name="ctx_attn",
    )(q, k, v)


def _gated_residual(x_ref, o_ref, z_ref, gate_ref, w_ref):
    z = z_ref[0].astype(F32)
    u = (o_ref[0].astype(F32) * (z * jax.nn.sigmoid(z))).astype(BF16)
    return x_ref[0] + gate_ref[0] * _dot(u, w_ref[...])


def _out_proj_kernel(x_ref, o_ref, z_ref, gate_ref, w_ref, fnw_ref, out_ref, *, final_norm):
    xn = _gated_residual(x_ref, o_ref, z_ref, gate_ref, w_ref)
    if final_norm:
        ms = jnp.mean(xn * xn, axis=-1, keepdims=True)
        xn = xn * lax.rsqrt(ms + EPS) * fnw_ref[...]
    out_ref[0] = xn


def _out_then_proj_kernel(x_ref, o_ref, z_ref, gate_ref, wo_ref, shift_ref, scale_ref, nw_ref, wi_ref,
                          x_out_ref, *out_refs):
    xn = _gated_residual(x_ref, o_ref, z_ref, gate_ref, wo_ref)
    x_out_ref[0] = xn
    _project(_modulated_norm(xn, nw_ref, shift_ref, scale_ref), wi_ref, out_refs)


def _out_then_proj(x, o, z, gate, w_out_bf16, shift, scale, norm_w, w_in_bf16, out_dtypes, name):
    bx, t, d = x.shape
    n = w_in_bf16.shape[1]
    tt = min(256, t)
    row = lambda width: pl.BlockSpec((1, tt, width), lambda b, i: (b, i, 0))
    mod_spec = pl.BlockSpec((1, 1, d), lambda b, i: (b, 0, 0))
    return pl.pallas_call(
        _out_then_proj_kernel,
        grid=(bx, t // tt),
        in_specs=[
            row(d), row(BRANCH), row(BRANCH), mod_spec,
            pl.BlockSpec((BRANCH, d), lambda b, i: (0, 0)),
            mod_spec, mod_spec,
            pl.BlockSpec((1, d), lambda b, i: (0, 0)),
            pl.BlockSpec((d, n), lambda b, i: (0, 0)),
        ],
        out_specs=[row(d)] + [row(BRANCH) for _ in out_dtypes],
        out_shape=[jax.ShapeDtypeStruct((bx, t, d), F32)]
        + [jax.ShapeDtypeStruct((bx, t, BRANCH), dt) for dt in out_dtypes],
        compiler_params=_params("arbitrary", "arbitrary"),
        name=name,
    )(x, o, z, gate, w_out_bf16, shift, scale, norm_w.reshape(1, d), w_in_bf16)


def _out_proj(x, o, z, gate, w_bf16, final_norm_w, final_norm, name):
    bx, t, d = x.shape
    tt = min(512, t)
    row = lambda width: pl.BlockSpec((1, tt, width), lambda b, i: (b, i, 0))
    return pl.pallas_call(
        functools.partial(_out_proj_kernel, final_norm=final_norm),
        grid=(bx, t // tt),
        in_specs=[
            row(d), row(BRANCH), row(BRANCH),
            pl.BlockSpec((1, 1, d), lambda b, i: (b, 0, 0)),
            pl.BlockSpec((BRANCH, d), lambda b, i: (0, 0)),
            pl.BlockSpec((1, d), lambda b, i: (0, 0)),
        ],
        out_specs=row(d),
        out_shape=jax.ShapeDtypeStruct((bx, t, d), F32),
        compiler_params=_params("arbitrary", "arbitrary"),
        name=name,
    )(x, o, z, gate, w_bf16, final_norm_w.reshape(1, d))


def _hgrn_kernel(q_ref, v_ref, ff_ref, fb_ref, vc_ref, ffc_ref, fbc_ref, lower_ref, hnw_ref, o_ref,
                 st_ref, o_dir_ref, dmask_ref, lmask_ref, sgn_ref, g_ref, p_ref, qk_ref, dl_ref, *, n_lat, n_ctx, layer):
    c = HG_CHUNK
    dk = HG_HEAD_DIM

    lw = lower_ref[...]
    e = jnp.exp(lw - jnp.max(lw, axis=0, keepdims=True))
    sm = e / jnp.sum(e, axis=0, keepdims=True)
    lb = jnp.zeros((1, dk), F32)
    for j in range(1, layer + 1):
        lb = lb + sm[j:j + 1]
    one_m_lb = 1.0 - lb

    row = lax.broadcasted_iota(jnp.int32, (c, c), 0)
    col = lax.broadcasted_iota(jnp.int32, (c, c), 1)
    rowl = lax.broadcasted_iota(jnp.int32, (c, dk), 0)
    tri = {False: jnp.tile((col <= row).astype(BF16), (1, 3)), True: jnp.tile((col >= row).astype(BF16), (1, 3))}

    for rev in (False, True):
        for dl in range(HG_DIAG):
            in_blk = (row & (HG_DIAG - 1)) + dl < HG_DIAG if rev else (row & (HG_DIAG - 1)) >= dl
            hit = col == (row + dl if rev else row - dl)
            dmask_ref[int(rev), dl] = (hit & in_blk).astype(F32)
        for li, half in enumerate(HG_LEVELS):
            shift = int(np.log2(2 * half))
            same = (row >> shift) == (col >> shift)
            r2 = (row & (2 * half - 1)) >= half
            c2 = (col & (2 * half - 1)) >= half
            pair = (jnp.logical_not(r2) & c2) if rev else (r2 & jnp.logical_not(c2))
            lmask_ref[int(rev), li] = (same & pair).astype(F32)
            second = (rowl & (2 * half - 1)) >= half
            sgn_ref[int(rev), li] = jnp.where(second, -1.0, 1.0) if rev else jnp.where(second, 1.0, -1.0)

    def gates(raw):
        ex = jnp.exp(-jnp.abs(raw))
        r = 1.0 / (1.0 + ex)
        er = ex * r
        pos = raw >= 0
        f = lb + one_m_lb * jnp.where(pos, r, er)
        kk = one_m_lb * jnp.where(pos, er, r)
        return f, kk, jnp.log2(jnp.maximum(f, F32_TINY))

    def decay_sums(raw, rev):
        f, kk, logf = gates(raw)
        a = _dot(tri[rev], jnp.concatenate(_split3(logf), axis=0))
        return f, kk, a

    ctx_chunks = [(d, pl.ds((n_ctx - 1 - j if d else j) * c, c)) for d in (0, 1) for j in range(n_ctx)]
    sums = [decay_sums((fbc_ref if d else ffc_ref)[0, sl, :], bool(d)) for d, sl in ctx_chunks]
    terms = []
    for (d, sl), (_, kk, a) in zip(ctx_chunks, sums):
        last = a[0:1] if d else a[c - 1:c]
        terms.append((jnp.exp2(last), _dot_tn(vc_ref[0, sl, :], (kk * jnp.exp2(last - a)).astype(BF16))))
    for d in (0, 1):
        st = jnp.zeros((dk, dk), F32)
        for dlast, upd in terms[d * n_ctx:(d + 1) * n_ctx]:
            st = st * dlast + upd
        st_ref[d] = st

    def state_update(d, vb, kst, dlast):
        st = st_ref[d]
        st_ref[d] = st * dlast + _dot_tn(vb, kst)
        return st

    qscale = HG_HEAD_DIM ** -0.5

    def rows_of(d, step):
        cidx = (n_lat - 1 - step) if d else step
        if isinstance(cidx, int):
            return pl.ds(cidx * c, c)
        return pl.ds(pl.multiple_of(cidx * c, c), c)

    def stage_sums(d, step):
        raw_ref = fb_ref if d else ff_ref
        f, kk, a = decay_sums(raw_ref[0, rows_of(d, step), :], bool(d))
        g_ref[d, 0] = f
        g_ref[d, 1] = kk
        g_ref[d, 2] = a

    def stage_scores(d, step):
        rev = bool(d)
        f = g_ref[d, 0]
        kk = g_ref[d, 1]
        a = g_ref[d, 2]
        qr = q_ref[0, rows_of(d, step), :].astype(F32)
        qs = qr * jax.nn.sigmoid(qr) * qscale
        last = a[0:1] if rev else a[c - 1:c]
        qs_b = qs.astype(BF16)
        kk_b = kk.astype(BF16)
        p = None
        for li, half in enumerate(HG_LEVELS):
            pieces = []
            for blk in range(c // (2 * half)):
                rr = blk * 2 * half + (half - 1 if rev else half)
                pieces.append(jnp.broadcast_to(a[rr:rr + 1], (2 * half, dk)))
            aref = pieces[0] if len(pieces) == 1 else jnp.concatenate(pieces, axis=0)
            x = jnp.exp2((a - aref) * sgn_ref[d, li]).astype(BF16)
            pl_ = _dot_nt(qs_b * x, kk_b * x) * lmask_ref[d, li]
            p = pl_ if p is None else p + pl_
        f3 = f.reshape(c // SUBLANES, SUBLANES, dk)
        g3 = kk.reshape(c // SUBLANES, SUBLANES, dk)
        for dl in range(HG_DIAG):
            if dl > 0:
                g3 = f3 * pltpu.roll(g3, (SUBLANES - 1) if rev else 1, 1)
            pr = qs * g3.reshape(c, dk)
            p = p + jnp.sum(pr, axis=-1, keepdims=True) * dmask_ref[d, dl]
        p_ref[d] = p.astype(BF16)
        qk_ref[d, 0] = qs_b * jnp.exp2(a).astype(BF16)
        qk_ref[d, 1] = kk_b * jnp.exp2(last - a).astype(BF16)
        dl_ref[d] = jnp.broadcast_to(jnp.exp2(last), (8, dk))

    def stage_output(d, step):
        sl = rows_of(d, step)
        vb = v_ref[0, sl, :]
        pmat = p_ref[d]
        qa = qk_ref[d, 0]
        st = state_update(d, vb, qk_ref[d, 1], dl_ref[d, 0:1, :])
        o_dir_ref[d, sl, :] = _dot_nt(qa, st.astype(BF16)) + _dot(pmat, vb)

    def body(step_out, step_scores, step_sums):
        for d in (0, 1):
            if step_out is not None:
                stage_output(d, step_out)
            if step_scores is not None:
                stage_scores(d, step_scores)
            if step_sums is not None:
                stage_sums(d, step_sums)

    body(None, None, 0)
    body(None, 0, 1)

    def steady(j, carry):
        body(j - 2, j - 1, j)
        return carry

    first = 2 + (n_lat - 2) % HG_UNROLL
    for j in range(2, first):
        body(j - 2, j - 1, j)
    lax.fori_loop(first, n_lat, steady, 0, unroll=HG_UNROLL)
    body(n_lat - 2, n_lat - 1, None)
    body(n_lat - 1, None, None)

    o = o_dir_ref[0] + o_dir_ref[1]
    ms = jnp.mean(o * o, axis=-1, keepdims=True)
    o_ref[0] = (o * lax.rsqrt(ms + EPS) * hnw_ref[...]).astype(o_ref.dtype)


def _hgrn_scan(q, v, ff, fb, vc, ffc, fbc, hg_lower, head_norm_w, layer):
    b, l, e = q.shape
    ctx = vc.shape[1]
    depth = hg_lower.shape[0]
    assert l % HG_CHUNK == 0 and ctx % HG_CHUNK == 0
    lat = pl.BlockSpec((1, l, LANES), lambda bi, h: (bi, 0, h))
    cspec = pl.BlockSpec((1, ctx, LANES), lambda bi, h: (bi, 0, h))
    return pl.pallas_call(
        functools.partial(_hgrn_kernel, n_lat=l // HG_CHUNK, n_ctx=ctx // HG_CHUNK, layer=layer),
        grid=(b, e // LANES),
        in_specs=[lat, lat, lat, lat, cspec, cspec, cspec,
                  pl.BlockSpec((depth, LANES), lambda bi, h: (0, h)),
                  pl.BlockSpec((1, LANES), lambda bi, h: (0, 0))],
        out_specs=lat,
        out_shape=jax.ShapeDtypeStruct((b, l, e), BF16),
        scratch_shapes=[
            pltpu.VMEM((2, HG_HEAD_DIM, HG_HEAD_DIM), F32),
            pltpu.VMEM((2, l, LANES), F32),
            pltpu.VMEM((2, HG_DIAG, HG_CHUNK, HG_CHUNK), F32),
            pltpu.VMEM((2, len(HG_LEVELS), HG_CHUNK, HG_CHUNK), F32),
            pltpu.VMEM((2, len(HG_LEVELS), HG_CHUNK, LANES), F32),
            pltpu.VMEM((2, 3, HG_CHUNK, LANES), F32),
            pltpu.VMEM((2, HG_CHUNK, HG_CHUNK), BF16),
            pltpu.VMEM((2, 2, HG_CHUNK, LANES), BF16),
            pltpu.VMEM((2, 8, LANES), F32),
        ],
        compiler_params=_params("arbitrary", "arbitrary"),
        name="hgrn_scan",
    )(q, v, ff, fb, vc, ffc, fbc, hg_lower, head_norm_w.reshape(1, LANES))


def kernel(x, c, ctx, c_ctx, ada_w, ada_b, norm_w, na_w_in, na_rpb, na_w_out, hg_w_in, hg_lower, hg_norm_w,
           hg_w_out, final_norm_w):
    b, l, d = x.shape
    assert d == D_MODEL and HG_HEAD_DIM == LANES and 2 * NA_HEAD_DIM == LANES

    pad = (-(b + 1)) % 8
    cond = jnp.concatenate([c, c_ctx[None, :], jnp.zeros((pad, d), F32)], axis=0)
    mod = _adaln(cond, ada_w, ada_b)

    def mods(i):
        m = mod[i]
        lat = [m[:b, j * d:(j + 1) * d].reshape(b, 1, d) for j in range(3)]
        cx = [jnp.broadcast_to(m[b:b + 1, j * d:(j + 1) * d].reshape(1, 1, d), (b, 1, d)) for j in range(3)]
        return lat, cx

    (shift, scale, gate), (shift_c, scale_c, gate_c) = mods(0)
    w_in = na_w_in[0].astype(BF16)
    w_out = na_w_out[0].astype(BF16)
    q, k, v, z = _proj(x, shift, scale, norm_w[0], w_in, [BF16] * 4, "na_proj")
    qc, kc, vc, zc = _proj(ctx, shift_c, scale_c, norm_w[0], w_in, [BF16] * 4, "na_proj_ctx")
    o = _na_attention(q, k, v, kc, vc, _na_bias_table(na_rpb[0]))
    oc = _ctx_attention(qc, kc, vc)
    (shift1, scale1, gate1), (shift1_c, scale1_c, _) = mods(1)
    w_in1 = hg_w_in[0].astype(BF16)
    dts = [BF16, BF16, F32, F32, BF16]
    x1, q, i_lat, f_fwd, f_bwd, g = _out_then_proj(
        x, o, z, gate, w_out, shift1, scale1, norm_w[1], w_in1, dts, "na_out_hg_proj")
    _, _, i_ctx, fc_fwd, fc_bwd, _ = _out_then_proj(
        ctx, oc, zc, gate_c, w_out, shift1_c, scale1_c, norm_w[1], w_in1, dts, "na_out_hg_proj_ctx")

    o = _hgrn_scan(q, i_lat, f_fwd, f_bwd, i_ctx, fc_fwd, fc_bwd, hg_lower, hg_norm_w[0], 1)
    return _out_proj(x1, o, g, gate1, hg_w_out[0].astype(BF16), final_norm_w, True, "hg_out")
```

```python
import functools

import numpy as np
import jax
import jax.numpy as jnp
from jax import lax
from jax.experimental import pallas as pl
from jax.experimental.pallas import tpu as pltpu

D_MODEL = 1024
BRANCH = D_MODEL
GRID_W = 64
NA_HEADS = 16
NA_HEAD_DIM = BRANCH // NA_HEADS
NA_WIN_H = 8
NA_WIN_W = 16
HG_HEADS = 8
HG_HEAD_DIM = BRANCH // HG_HEADS
HG_CHUNK = 64
HG_DIAG = 4
HG_LEVELS = (32, 16, 8, 4)
HG_GROUP = 4
NA_UNROLL = 4
EPS = 1e-6
LANES = 128
SUBLANES = 8
F32_TINY = 1e-37
VMEM_LIMIT = 56 * 1024 * 1024
PROJ_TN = 512

BF16 = jnp.bfloat16
F32 = jnp.float32
NT_DIMS = (((1,), (1,)), ((), ()))
TN_DIMS = (((0,), (0,)), ((), ()))


def _dot(a, b):
    return jnp.dot(a, b, preferred_element_type=F32)


def _dot_nt(a, b):
    return lax.dot_general(a, b, NT_DIMS, preferred_element_type=F32)


def _dot_tn(a, b):
    return lax.dot_general(a, b, TN_DIMS, preferred_element_type=F32)


def _split2(x):
    hi = x.astype(BF16)
    return hi, (x - hi.astype(F32)).astype(BF16)


def _params(*sem):
    return pltpu.CompilerParams(dimension_semantics=sem, vmem_limit_bytes=VMEM_LIMIT)


def _adaln_kernel(cond_ref, w_ref, b_ref, out_ref):
    c = cond_ref[...]
    a = c * jax.nn.sigmoid(c)
    w = w_ref[0]
    a_hi = a.astype(BF16)
    a_lo = (a - a_hi.astype(F32)).astype(BF16)
    w_hi = w.astype(BF16)
    w_lo = (w - w_hi.astype(F32)).astype(BF16)
    acc = _dot(a_hi, w_hi) + _dot(a_hi, w_lo) + _dot(a_lo, w_hi)
    out_ref[0] = acc + b_ref[0]


def _adaln(cond, ada_w, ada_b):
    depth, d, n = ada_w.shape
    rows = cond.shape[0]
    tn = 512
    return pl.pallas_call(
        _adaln_kernel,
        grid=(depth, n // tn),
        in_specs=[
            pl.BlockSpec((rows, d), lambda l, j: (0, 0)),
            pl.BlockSpec((1, d, tn), lambda l, j: (l, 0, j)),
            pl.BlockSpec((1, 1, tn), lambda l, j: (l, 0, j)),
        ],
        out_specs=pl.BlockSpec((1, rows, tn), lambda l, j: (l, 0, j)),
        out_shape=jax.ShapeDtypeStruct((depth, rows, n), F32),
        compiler_params=_params("arbitrary", "arbitrary"),
        name="adaln",
    )(cond, ada_w, ada_b.reshape(depth, 1, n))


def _modulated_norm(x, nw_ref, shift_ref, scale_ref):
    ms = jnp.mean(x * x, axis=-1, keepdims=True)
    y = x * lax.rsqrt(ms + EPS) * nw_ref[...]
    return (y * (1.0 + scale_ref[0]) + shift_ref[0]).astype(BF16)


def _project(h, w_ref, out_refs):
    for g, o_ref in enumerate(out_refs):
        for j in range(BRANCH // PROJ_TN):
            c0 = g * BRANCH + j * PROJ_TN
            o_ref[0, :, j * PROJ_TN:(j + 1) * PROJ_TN] = _dot(h, w_ref[:, c0:c0 + PROJ_TN]).astype(o_ref.dtype)


def _proj_kernel(x_ref, shift_ref, scale_ref, nw_ref, w_ref, *out_refs):
    _project(_modulated_norm(x_ref[0], nw_ref, shift_ref, scale_ref), w_ref, out_refs)


def _proj(x, shift, scale, norm_w, w_bf16, out_dtypes, name):
    bx, t, d = x.shape
    n = w_bf16.shape[1]
    tt = min(256, t)
    row_spec = pl.BlockSpec((1, tt, d), lambda b, i: (b, i, 0))
    mod_spec = pl.BlockSpec((1, 1, d), lambda b, i: (b, 0, 0))
    return pl.pallas_call(
        _proj_kernel,
        grid=(bx, t // tt),
        in_specs=[
            row_spec, mod_spec, mod_spec,
            pl.BlockSpec((1, d), lambda b, i: (0, 0)),
            pl.BlockSpec((d, n), lambda b, i: (0, 0)),
        ],
        out_specs=[pl.BlockSpec((1, tt, BRANCH), lambda b, i: (b, i, 0)) for _ in out_dtypes],
        out_shape=[jax.ShapeDtypeStruct((bx, t, BRANCH), dt) for dt in out_dtypes],
        compiler_params=_params("arbitrary", "arbitrary"),
        name=name,
    )(x, shift, scale, norm_w.reshape(1, d), w_bf16)


def _na_bias_table(rpb):
    cols = np.arange(GRID_W)
    col_start = np.clip(cols - NA_WIN_W // 2, 0, GRID_W - NA_WIN_W)
    kc = cols[None, :]
    qc = cols[:, None]
    in_window = (kc >= col_start[:, None]) & (kc < col_start[:, None] + NA_WIN_W)
    dc = np.clip(kc - qc + NA_WIN_W - 1, 0, 2 * NA_WIN_W - 2)
    t15 = jnp.where(jnp.asarray(in_window)[None, None], rpb[:, :, jnp.asarray(dc)], -jnp.inf)
    return jnp.concatenate([t15[:, :-1], t15[:, 1:]], axis=-1).astype(F32)


def _head_pair_select(o2, lane_lo):
    m = o2.shape[0] // 2
    return jnp.where(lane_lo, o2[:m], o2[m:])


def _na_kernel(q_ref, k_ref, v_ref, kc_ref, vc_ref, tp_ref, o_ref, s_ref, o2_ref, inv_ref, *, rows):
    lane_lo = lax.broadcasted_iota(jnp.int32, (1, LANES), 1) < NA_HEAD_DIM
    kc = kc_ref[0]
    vc = vc_ref[0]
    kh = NA_WIN_H
    n_loc = kh * GRID_W
    scale = NA_HEAD_DIM ** -0.5

    def window(r):
        if isinstance(r, int):
            r0 = min(max(r - kh // 2, 0), rows - kh)
            return pl.ds(r0 * GRID_W, n_loc), r0 - r + (NA_WIN_H - 1)
        r0 = jnp.clip(r - kh // 2, 0, rows - kh)
        return pl.ds(pl.multiple_of(r0 * GRID_W, GRID_W), n_loc), r0 - r + (NA_WIN_H - 1)

    def query_rows(r):
        if isinstance(r, int):
            return pl.ds(r * GRID_W, GRID_W)
        return pl.ds(pl.multiple_of(r * GRID_W, GRID_W), GRID_W)

    def scores(r, slot):
        q = q_ref[0, query_rows(r), :] * scale
        zero = jnp.zeros_like(q)
        qq = jnp.concatenate([jnp.where(lane_lo, q, zero), jnp.where(lane_lo, zero, q)], axis=0)
        keys, _ = window(r)
        s_ref[slot, :, :n_loc] = _dot_nt(qq, k_ref[0, keys, :])
        s_ref[slot, :, n_loc:] = _dot_nt(qq, kc)

    def softmax_pv(r, slot):
        keys, dr0 = window(r)
        bias = jnp.concatenate(
            [jnp.concatenate([tp_ref[0, dr0 + 2 * i], tp_ref[1, dr0 + 2 * i]], axis=0) for i in range(kh // 2)],
            axis=1)
        s_loc = s_ref[slot, :, :n_loc] + bias
        s_ctx = s_ref[slot, :, n_loc:]
        m = jnp.maximum(jnp.max(s_loc, axis=-1, keepdims=True), jnp.max(s_ctx, axis=-1, keepdims=True))
        p_loc = jnp.exp(s_loc - m)
        p_ctx = jnp.exp(s_ctx - m)
        denom = jnp.sum(p_loc, axis=-1, keepdims=True) + jnp.sum(p_ctx, axis=-1, keepdims=True)
        o2_ref[slot] = _dot(p_loc.astype(BF16), v_ref[0, keys, :]) + _dot(p_ctx.astype(BF16), vc)
        inv_ref[slot] = 1.0 / denom

    def finish(r, slot):
        o2 = o2_ref[slot] * inv_ref[slot]
        o_ref[0, query_rows(r), :] = _head_pair_select(o2, lane_lo).astype(o_ref.dtype)

    u = NA_UNROLL
    for i in range(u):
        scores(i, i)
    for i in range(u):
        softmax_pv(i, i)
        scores(u + i, i)

    def steady(t, carry):
        r = t * u
        for i in range(u):
            finish(r - 2 * u + i, i)
            softmax_pv(r - u + i, i)
            scores(r + i, i)
        return carry

    lax.fori_loop(2, rows // u, steady, 0)
    for i in range(u):
        finish(rows - 2 * u + i, i)
        softmax_pv(rows - u + i, i)
    for i in range(u):
        finish(rows - u + i, i)


def _na_attention(q, k, v, kc, vc, tp):
    b, l, e = q.shape
    ctx = kc.shape[1]
    rows = l // GRID_W
    assert rows >= NA_WIN_H and l % GRID_W == 0
    assert rows % NA_UNROLL == 0 and rows // NA_UNROLL >= 2
    n_pairs = e // LANES
    lat = pl.BlockSpec((1, l, LANES), lambda hp, bi: (bi, 0, hp))
    cspec = pl.BlockSpec((1, ctx, LANES), lambda hp, bi: (bi, 0, hp))
    return pl.pallas_call(
        functools.partial(_na_kernel, rows=rows),
        grid=(n_pairs, b),
        in_specs=[lat, lat, lat, cspec, cspec,
                  pl.BlockSpec((2, 2 * NA_WIN_H - 2, GRID_W, LANES), lambda hp, bi: (hp, 0, 0, 0))],
        out_specs=lat,
        out_shape=jax.ShapeDtypeStruct((b, l, e), BF16),
        scratch_shapes=[
            pltpu.VMEM((NA_UNROLL, 2 * GRID_W, NA_WIN_H * GRID_W + ctx), F32),
            pltpu.VMEM((NA_UNROLL, 2 * GRID_W, LANES), F32),
            pltpu.VMEM((NA_UNROLL, 2 * GRID_W, 1), F32),
        ],
        compiler_params=_params("arbitrary", "arbitrary"),
        name="na_attn",
    )(q, k, v, kc, vc, tp)


def _ctx_attn_kernel(q_ref, k_ref, v_ref, o_ref):
    lane_lo = lax.broadcasted_iota(jnp.int32, (1, LANES), 1) < NA_HEAD_DIM
    for hp in range(q_ref.shape[-1] // LANES):
        cols = slice(hp * LANES, (hp + 1) * LANES)
        q = q_ref[0, :, cols] * (NA_HEAD_DIM ** -0.5)
        zero = jnp.zeros_like(q)
        qq = jnp.concatenate([jnp.where(lane_lo, q, zero), jnp.where(lane_lo, zero, q)], axis=0)
        s = _dot_nt(qq, k_ref[0, :, cols])
        m = jnp.max(s, axis=-1, keepdims=True)
        p = jnp.exp(s - m)
        o2 = _dot(p.astype(BF16), v_ref[0, :, cols]) / jnp.sum(p, axis=-1, keepdims=True)
        o_ref[0, :, cols] = _head_pair_select(o2, lane_lo).astype(o_ref.dtype)


def _ctx_attention(q, k, v):
    b, ctx, e = q.shape
    spec = pl.BlockSpec((1, ctx, e), lambda bi: (bi, 0, 0))
    return pl.pallas_call(
        _ctx_attn_kernel,
        grid=(b,),
        in_specs=[spec, spec, spec],
        out_specs=spec,
        out_shape=jax.ShapeDtypeStruct((b, ctx, e), BF16),
        compiler_params=_params("arbitrary"),
        name="ctx_attn",
    )(q, k, v)


def _gated_residual(x_ref, o_ref, z_ref, gate_ref, w_ref):
    z = z_ref[0].astype(F32)
    u = (o_ref[0].astype(F32) * (z * jax.nn.sigmoid(z))).astype(BF16)
    return x_ref[0] + gate_ref[0] * _dot(u, w_ref[...])


def _out_proj_kernel(x_ref, o_ref, z_ref, gate_ref, w_ref, fnw_ref, out_ref, *, final_norm):
    xn = _gated_residual(x_ref, o_ref, z_ref, gate_ref, w_ref)
    if final_norm:
        ms = jnp.mean(xn * xn, axis=-1, keepdims=True)
        xn = xn * lax.rsqrt(ms + EPS) * fnw_ref[...]
    out_ref[0] = xn


def _out_then_proj_kernel(x_ref, o_ref, z_ref, gate_ref, wo_ref, shift_ref, scale_ref, nw_ref, wi_ref,
                          x_out_ref, *out_refs):
    xn = _gated_residual(x_ref, o_ref, z_ref, gate_ref, wo_ref)
    x_out_ref[0] = xn
    _project(_modulated_norm(xn, nw_ref, shift_ref, scale_ref), wi_ref, out_refs)


def _out_then_proj(x, o, z, gate, w_out_bf16, shift, scale, norm_w, w_in_bf16, out_dtypes, name):
    bx, t, d = x.shape
    n = w_in_bf16.shape[1]
    tt = min(256, t)
    row = lambda width: pl.BlockSpec((1, tt, width), lambda b, i: (b, i, 0))
    mod_spec = pl.BlockSpec((1, 1, d), lambda b, i: (b, 0, 0))
    return pl.pallas_call(
        _out_then_proj_kernel,
        grid=(bx, t // tt),
        in_specs=[
            row(d), row(BRANCH), row(BRANCH), mod_spec,
            pl.BlockSpec((BRANCH, d), lambda b, i: (0, 0)),
            mod_spec, mod_spec,
            pl.BlockSpec((1, d), lambda b, i: (0, 0)),
            pl.BlockSpec((d, n), lambda b, i: (0, 0)),
        ],
        out_specs=[row(d)] + [row(BRANCH) for _ in out_dtypes],
        out_shape=[jax.ShapeDtypeStruct((bx, t, d), F32)]
        + [jax.ShapeDtypeStruct((bx, t, BRANCH), dt) for dt in out_dtypes],
        compiler_params=_params("arbitrary", "arbitrary"),
        name=name,
    )(x, o, z, gate, w_out_bf16, shift, scale, norm_w.reshape(1, d), w_in_bf16)


def _out_proj(x, o, z, gate, w_bf16, final_norm_w, final_norm, name):
    bx, t, d = x.shape
    tt = min(512, t)
    row = lambda width: pl.BlockSpec((1, tt, width), lambda b, i: (b, i, 0))
    return pl.pallas_call(
        functools.partial(_out_proj_kernel, final_norm=final_norm),
        grid=(bx, t // tt),
        in_specs=[
            row(d), row(BRANCH), row(BRANCH),
            pl.BlockSpec((1, 1, d), lambda b, i: (b, 0, 0)),
            pl.BlockSpec((BRANCH, d), lambda b, i: (0, 0)),
            pl.BlockSpec((1, d), lambda b, i: (0, 0)),
        ],
        out_specs=row(d),
        out_shape=jax.ShapeDtypeStruct((bx, t, d), F32),
        compiler_params=_params("arbitrary", "arbitrary"),
        name=name,
    )(x, o, z, gate, w_bf16, final_norm_w.reshape(1, d))


def _hgrn_kernel(q_ref, v_ref, ff_ref, fb_ref, vc_ref, ffc_ref, fbc_ref, lower_ref, hnw_ref, o_ref,
                 st_ref, o_dir_ref, dmask_ref, lmask_ref, sgn_ref, g_ref, p_ref, qk_ref, dl_ref, *, n_lat, n_ctx, layer):
    c = HG_CHUNK
    dk = HG_HEAD_DIM

    lw = lower_ref[...]
    e = jnp.exp(lw - jnp.max(lw, axis=0, keepdims=True))
    sm = e / jnp.sum(e, axis=0, keepdims=True)
    lb = jnp.zeros((1, dk), F32)
    for j in range(1, layer + 1):
        lb = lb + sm[j:j + 1]
    one_m_lb = 1.0 - lb

    row = lax.broadcasted_iota(jnp.int32, (c, c), 0)
    col = lax.broadcasted_iota(jnp.int32, (c, c), 1)
    rowl = lax.broadcasted_iota(jnp.int32, (c, dk), 0)
    tri = {False: jnp.tile((col <= row).astype(BF16), (1, 2)), True: jnp.tile((col >= row).astype(BF16), (1, 2))}

    for rev in (False, True):
        for dl in range(HG_DIAG):
            in_blk = (row & (HG_DIAG - 1)) + dl < HG_DIAG if rev else (row & (HG_DIAG - 1)) >= dl
            hit = col == (row + dl if rev else row - dl)
            dmask_ref[int(rev), dl] = (hit & in_blk).astype(F32)
        for li, half in enumerate(HG_LEVELS):
            shift = int(np.log2(2 * half))
            same = (row >> shift) == (col >> shift)
            r2 = (row & (2 * half - 1)) >= half
            c2 = (col & (2 * half - 1)) >= half
            pair = (jnp.logical_not(r2) & c2) if rev else (r2 & jnp.logical_not(c2))
            lmask_ref[int(rev), li] = (same & pair).astype(F32)
            second = (rowl & (2 * half - 1)) >= half
            sgn_ref[int(rev), li] = jnp.where(second, -1.0, 1.0) if rev else jnp.where(second, 1.0, -1.0)

    def gates(raw):
        f = lb + one_m_lb * jax.nn.sigmoid(raw)
        return f, 1.0 - f, jnp.log2(jnp.maximum(f, F32_TINY))

    def decay_sums(raw, rev):
        f, kk, logf = gates(raw)
        a = _dot(tri[rev], jnp.concatenate(_split2(logf), axis=0))
        return f, kk, a

    ctx_chunks = [(d, pl.ds((n_ctx - 1 - j if d else j) * c, c)) for d in (0, 1) for j in range(n_ctx)]
    ctx_sums = [decay_sums((fbc_ref if d else ffc_ref)[0, sl, :], bool(d)) for d, sl in ctx_chunks]

    def ctx_states():
        terms = []
        for (d, sl), (_, kk, a) in zip(ctx_chunks, ctx_sums):
            last = a[0:1] if d else a[c - 1:c]
            terms.append((jnp.exp2(last), _dot_tn(vc_ref[0, sl, :], (kk * jnp.exp2(last - a)).astype(BF16))))
        for d in (0, 1):
            st = jnp.zeros((dk, dk), F32)
            for dlast, upd in terms[d * n_ctx:(d + 1) * n_ctx]:
                st = st * dlast + upd
            st_ref[d] = st

    def state_update(d, vb, kst, dlast):
        st = st_ref[d]
        st_ref[d] = st * dlast + _dot_tn(vb, kst)
        return st

    qscale = HG_HEAD_DIM ** -0.5

    def rows_of(d, step):
        cidx = (n_lat - 1 - step) if d else step
        if isinstance(cidx, int):
            return pl.ds(cidx * c, c)
        return pl.ds(pl.multiple_of(cidx * c, c), c)

    def stage_sums(d, step, slot):
        raw_ref = fb_ref if d else ff_ref
        f, kk, a = decay_sums(raw_ref[0, rows_of(d, step), :], bool(d))
        g_ref[d, slot, 0] = f
        g_ref[d, slot, 1] = kk
        g_ref[d, slot, 2] = a

    def stage_scores(d, step, slot):
        rev = bool(d)
        f = g_ref[d, slot, 0]
        kk = g_ref[d, slot, 1]
        a = g_ref[d, slot, 2]
        qr = q_ref[0, rows_of(d, step), :].astype(F32)
        qs = qr * jax.nn.sigmoid(qr) * qscale
        last = a[0:1] if rev else a[c - 1:c]
        qs_b = qs.astype(BF16)
        kk_b = kk.astype(BF16)
        p = None
        for li, half in enumerate(HG_LEVELS):
            pieces = []
            for blk in range(c // (2 * half)):
                rr = blk * 2 * half + (half - 1 if rev else half)
                pieces.append(jnp.broadcast_to(a[rr:rr + 1], (2 * half, dk)))
            aref = pieces[0] if len(pieces) == 1 else jnp.concatenate(pieces, axis=0)
            x = jnp.exp2((a - aref) * sgn_ref[d, li]).astype(BF16)
            pl_ = _dot_nt(qs_b * x, kk_b * x) * lmask_ref[d, li]
            p = pl_ if p is None else p + pl_
        f3 = f.reshape(c // SUBLANES, SUBLANES, dk)
        g3 = kk.reshape(c // SUBLANES, SUBLANES, dk)
        for dl in range(HG_DIAG):
            if dl > 0:
                g3 = f3 * pltpu.roll(g3, (SUBLANES - 1) if rev else 1, 1)
            pr = qs * g3.reshape(c, dk)
            p = p + jnp.sum(pr, axis=-1, keepdims=True) * dmask_ref[d, dl]
        p_ref[d, slot] = p.astype(BF16)
        qk_ref[d, slot, 0] = qs_b * jnp.exp2(a).astype(BF16)
        qk_ref[d, slot, 1] = kk_b * jnp.exp2(last - a).astype(BF16)
        dl_ref[d, slot] = jnp.broadcast_to(jnp.exp2(last), (8, dk))

    def stage_output(d, step, slot):
        sl = rows_of(d, step)
        vb = v_ref[0, sl, :]
        pmat = p_ref[d, slot]
        qa = qk_ref[d, slot, 0]
        st = state_update(d, vb, qk_ref[d, slot, 1], dl_ref[d, slot, 0:1, :])
        o_dir_ref[d, sl, :] = _dot_nt(qa, st.astype(BF16)) + _dot(pmat, vb)

    grp = HG_GROUP

    def body(s_out, s_scores, s_sums):
        for i in range(grp):
            for d in (0, 1):
                if s_out is not None:
                    stage_output(d, s_out + i, i)
                if s_scores is not None:
                    stage_scores(d, s_scores + i, i)
                if s_sums is not None:
                    stage_sums(d, s_sums + i, i)

    body(None, None, 0)
    ctx_states()
    body(None, 0, grp)

    def steady(t, carry):
        s0 = t * grp
        body(s0 - 2 * grp, s0 - grp, s0)
        return carry

    lax.fori_loop(2, n_lat // grp, steady, 0)
    body(n_lat - 2 * grp, n_lat - grp, None)
    body(n_lat - grp, None, None)

    o = o_dir_ref[0] + o_dir_ref[1]
    ms = jnp.mean(o * o, axis=-1, keepdims=True)
    o_ref[0] = (o * lax.rsqrt(ms + EPS) * hnw_ref[...]).astype(o_ref.dtype)


def _hgrn_scan(q, v, ff, fb, vc, ffc, fbc, hg_lower, head_norm_w, layer):
    b, l, e = q.shape
    ctx = vc.shape[1]
    depth = hg_lower.shape[0]
    assert l % (HG_CHUNK * HG_GROUP) == 0 and l // (HG_CHUNK * HG_GROUP) >= 2 and ctx % HG_CHUNK == 0
    lat = pl.BlockSpec((1, l, LANES), lambda bi, h: (bi, 0, h))
    cspec = pl.BlockSpec((1, ctx, LANES), lambda bi, h: (bi, 0, h))
    return pl.pallas_call(
        functools.partial(_hgrn_kernel, n_lat=l // HG_CHUNK, n_ctx=ctx // HG_CHUNK, layer=layer),
        grid=(b, e // LANES),
        in_specs=[lat, lat, lat, lat, cspec, cspec, cspec,
                  pl.BlockSpec((depth, LANES), lambda bi, h: (0, h)),
                  pl.BlockSpec((1, LANES), lambda bi, h: (0, 0))],
        out_specs=lat,
        out_shape=jax.ShapeDtypeStruct((b, l, e), BF16),
        scratch_shapes=[
            pltpu.VMEM((2, HG_HEAD_DIM, HG_HEAD_DIM), F32),
            pltpu.VMEM((2, l, LANES), F32),
            pltpu.VMEM((2, HG_DIAG, HG_CHUNK, HG_CHUNK), F32),
            pltpu.VMEM((2, len(HG_LEVELS), HG_CHUNK, HG_CHUNK), F32),
            pltpu.VMEM((2, len(HG_LEVELS), HG_CHUNK, LANES), F32),
            pltpu.VMEM((2, HG_GROUP, 3, HG_CHUNK, LANES), F32),
            pltpu.VMEM((2, HG_GROUP, HG_CHUNK, HG_CHUNK), BF16),
            pltpu.VMEM((2, HG_GROUP, 2, HG_CHUNK, LANES), BF16),
            pltpu.VMEM((2, HG_GROUP, 8, LANES), F32),
        ],
        compiler_params=_params("arbitrary", "arbitrary"),
        name="hgrn_scan",
    )(q, v, ff, fb, vc, ffc, fbc, hg_lower, head_norm_w.reshape(1, LANES))


def kernel(x, c, ctx, c_ctx, ada_w, ada_b, norm_w, na_w_in, na_rpb, na_w_out, hg_w_in, hg_lower, hg_norm_w,
           hg_w_out, final_norm_w):
    b, l, d = x.shape
    assert d == D_MODEL and HG_HEAD_DIM == LANES and 2 * NA_HEAD_DIM == LANES

    pad = (-(b + 1)) % 8
    cond = jnp.concatenate([c, c_ctx[None, :], jnp.zeros((pad, d), F32)], axis=0)
    mod = _adaln(cond, ada_w, ada_b)

    def mods(i):
        m = mod[i]
        lat = [m[:b, j * d:(j + 1) * d].reshape(b, 1, d) for j in range(3)]
        cx = [jnp.broadcast_to(m[b:b + 1, j * d:(j + 1) * d].reshape(1, 1, d), (b, 1, d)) for j in range(3)]
        return lat, cx

    (shift, scale, gate), (shift_c, scale_c, gate_c) = mods(0)
    w_in = na_w_in[0].astype(BF16)
    w_out = na_w_out[0].astype(BF16)
    q, k, v, z = _proj(x, shift, scale, norm_w[0], w_in, [BF16] * 4, "na_proj")
    qc, kc, vc, zc = _proj(ctx, shift_c, scale_c, norm_w[0], w_in, [BF16] * 4, "na_proj_ctx")
    o = _na_attention(q, k, v, kc, vc, _na_bias_table(na_rpb[0]))
    oc = _ctx_attention(qc, kc, vc)
    (shift1, scale1, gate1), (shift1_c, scale1_c, _) = mods(1)
    w_in1 = hg_w_in[0].astype(BF16)
    dts = [BF16, BF16, F32, F32, BF16]
    x1, q, i_lat, f_fwd, f_bwd, g = _out_then_proj(
        x, o, z, gate, w_out, shift1, scale1, norm_w[1], w_in1, dts, "na_out_hg_proj")
    _, _, i_ctx, fc_fwd, fc_bwd, _ = _out_then_proj(
        ctx, oc, zc, gate_c, w_out, shift1_c, scale1_c, norm_w[1], w_in1, dts, "na_out_hg_proj_ctx")

    o = _hgrn_scan(q, i_lat, f_fwd, f_bwd, i_ctx, fc_fwd, fc_bwd, hg_lower, hg_norm_w[0], 1)
    return _out_proj(x1, o, g, gate1, hg_w_out[0].astype(BF16), final_norm_w, True, "hg_out")
```

```python
import functools

import numpy as np
import jax
import jax.numpy as jnp
from jax import lax
from jax.experimental import pallas as pl
from jax.experimental.pallas import tpu as pltpu

D_MODEL = 1024
BRANCH = D_MODEL
GRID_W = 64
NA_HEADS = 16
NA_HEAD_DIM = BRANCH // NA_HEADS
NA_WIN_H = 8
NA_WIN_W = 16
HG_HEADS = 8
HG_HEAD_DIM = BRANCH // HG_HEADS
HG_CHUNK = 64
HG_DIAG = 4
HG_LEVELS = (32, 16, 8, 4)
HG_GROUP = 8
NA_UNROLL = 4
EPS = 1e-6
LANES = 128
SUBLANES = 8
F32_TINY = 1e-37
VMEM_LIMIT = 56 * 1024 * 1024
PROJ_TN = 512

BF16 = jnp.bfloat16
F32 = jnp.float32
NT_DIMS = (((1,), (1,)), ((), ()))
TN_DIMS = (((0,), (0,)), ((), ()))


def _dot(a, b):
    return jnp.dot(a, b, preferred_element_type=F32)


def _dot_nt(a, b):
    return lax.dot_general(a, b, NT_DIMS, preferred_element_type=F32)


def _dot_tn(a, b):
    return lax.dot_general(a, b, TN_DIMS, preferred_element_type=F32)


def _split2(x):
    hi = x.astype(BF16)
    return hi, (x - hi.astype(F32)).astype(BF16)


def _params(*sem):
    return pltpu.CompilerParams(dimension_semantics=sem, vmem_limit_bytes=VMEM_LIMIT)


def _adaln_kernel(cond_ref, w_ref, b_ref, out_ref):
    c = cond_ref[...]
    a = c * jax.nn.sigmoid(c)
    w = w_ref[0]
    a_hi = a.astype(BF16)
    a_lo = (a - a_hi.astype(F32)).astype(BF16)
    w_hi = w.astype(BF16)
    w_lo = (w - w_hi.astype(F32)).astype(BF16)
    acc = _dot(a_hi, w_hi) + _dot(a_hi, w_lo) + _dot(a_lo, w_hi)
    out_ref[0] = acc + b_ref[0]


def _adaln(cond, ada_w, ada_b):
    depth, d, n = ada_w.shape
    rows = cond.shape[0]
    tn = 512
    return pl.pallas_call(
        _adaln_kernel,
        grid=(depth, n // tn),
        in_specs=[
            pl.BlockSpec((rows, d), lambda l, j: (0, 0)),
            pl.BlockSpec((1, d, tn), lambda l, j: (l, 0, j)),
            pl.BlockSpec((1, 1, tn), lambda l, j: (l, 0, j)),
        ],
        out_specs=pl.BlockSpec((1, rows, tn), lambda l, j: (l, 0, j)),
        out_shape=jax.ShapeDtypeStruct((depth, rows, n), F32),
        compiler_params=_params("arbitrary", "arbitrary"),
        name="adaln",
    )(cond, ada_w, ada_b.reshape(depth, 1, n))


def _modulated_norm(x, nw_ref, shift_ref, scale_ref):
    ms = jnp.mean(x * x, axis=-1, keepdims=True)
    y = x * lax.rsqrt(ms + EPS) * nw_ref[...]
    return (y * (1.0 + scale_ref[0]) + shift_ref[0]).astype(BF16)


def _project(h, w_ref, out_refs):
    for g, o_ref in enumerate(out_refs):
        for j in range(BRANCH // PROJ_TN):
            c0 = g * BRANCH + j * PROJ_TN
            o_ref[0, :, j * PROJ_TN:(j + 1) * PROJ_TN] = _dot(h, w_ref[:, c0:c0 + PROJ_TN]).astype(o_ref.dtype)


def _proj_kernel(x_ref, shift_ref, scale_ref, nw_ref, w_ref, *out_refs):
    _project(_modulated_norm(x_ref[0], nw_ref, shift_ref, scale_ref), w_ref, out_refs)


def _proj(x, shift, scale, norm_w, w_bf16, out_dtypes, name):
    bx, t, d = x.shape
    n = w_bf16.shape[1]
    tt = min(512, t)
    row_spec = pl.BlockSpec((1, tt, d), lambda b, i: (b, i, 0))
    mod_spec = pl.BlockSpec((1, 1, d), lambda b, i: (b, 0, 0))
    return pl.pallas_call(
        _proj_kernel,
        grid=(bx, t // tt),
        in_specs=[
            row_spec, mod_spec, mod_spec,
            pl.BlockSpec((1, d), lambda b, i: (0, 0)),
            pl.BlockSpec((d, n), lambda b, i: (0, 0)),
        ],
        out_specs=[pl.BlockSpec((1, tt, BRANCH), lambda b, i: (b, i, 0)) for _ in out_dtypes],
        out_shape=[jax.ShapeDtypeStruct((bx, t, BRANCH), dt) for dt in out_dtypes],
        compiler_params=_params("arbitrary", "arbitrary"),
        name=name,
    )(x, shift, scale, norm_w.reshape(1, d), w_bf16)


def _na_bias_table(rpb):
    cols = np.arange(GRID_W)
    col_start = np.clip(cols - NA_WIN_W // 2, 0, GRID_W - NA_WIN_W)
    kc = cols[None, :]
    qc = cols[:, None]
    in_window = (kc >= col_start[:, None]) & (kc < col_start[:, None] + NA_WIN_W)
    dc = np.clip(kc - qc + NA_WIN_W - 1, 0, 2 * NA_WIN_W - 2)
    t15 = jnp.where(jnp.asarray(in_window)[None, None], rpb[:, :, jnp.asarray(dc)], -jnp.inf)
    return jnp.concatenate([t15[:, :-1], t15[:, 1:]], axis=-1).astype(F32)


def _head_pair_select(o2, lane_lo):
    m = o2.shape[0] // 2
    return jnp.where(lane_lo, o2[:m], o2[m:])


def _na_kernel(q_ref, k_ref, v_ref, kc_ref, vc_ref, tp_ref, o_ref, s_ref, o2_ref, inv_ref, *, rows):
    lane_lo = lax.broadcasted_iota(jnp.int32, (1, LANES), 1) < NA_HEAD_DIM
    kc = kc_ref[0]
    vc = vc_ref[0]
    kh = NA_WIN_H
    n_loc = kh * GRID_W
    scale = NA_HEAD_DIM ** -0.5

    def window(r):
        if isinstance(r, int):
            r0 = min(max(r - kh // 2, 0), rows - kh)
            return pl.ds(r0 * GRID_W, n_loc), r0 - r + (NA_WIN_H - 1)
        r0 = jnp.clip(r - kh // 2, 0, rows - kh)
        return pl.ds(pl.multiple_of(r0 * GRID_W, GRID_W), n_loc), r0 - r + (NA_WIN_H - 1)

    def query_rows(r):
        if isinstance(r, int):
            return pl.ds(r * GRID_W, GRID_W)
        return pl.ds(pl.multiple_of(r * GRID_W, GRID_W), GRID_W)

    def scores(r, slot, par):
        q = q_ref[0, query_rows(r), :] * scale
        zero = jnp.zeros_like(q)
        qq = jnp.concatenate([jnp.where(lane_lo, q, zero), jnp.where(lane_lo, zero, q)], axis=0)
        keys, _ = window(r)
        s_ref[par, slot, :, :n_loc] = _dot_nt(qq, k_ref[0, keys, :])
        s_ref[par, slot, :, n_loc:] = _dot_nt(qq, kc)

    def softmax_pv(r, slot, par):
        keys, dr0 = window(r)
        bias = jnp.concatenate(
            [jnp.concatenate([tp_ref[0, dr0 + 2 * i], tp_ref[1, dr0 + 2 * i]], axis=0) for i in range(kh // 2)],
            axis=1)
        s_loc = s_ref[par, slot, :, :n_loc] + bias
        s_ctx = s_ref[par, slot, :, n_loc:]
        m = jnp.maximum(jnp.max(s_loc, axis=-1, keepdims=True), jnp.max(s_ctx, axis=-1, keepdims=True))
        p_loc = jnp.exp(s_loc - m)
        p_ctx = jnp.exp(s_ctx - m)
        denom = jnp.sum(p_loc, axis=-1, keepdims=True) + jnp.sum(p_ctx, axis=-1, keepdims=True)
        o2_ref[slot] = _dot(p_loc.astype(BF16), v_ref[0, keys, :]) + _dot(p_ctx.astype(BF16), vc)
        inv_ref[slot] = 1.0 / denom

    def finish(r, slot):
        o2 = o2_ref[slot] * inv_ref[slot]
        o_ref[0, query_rows(r), :] = _head_pair_select(o2, lane_lo).astype(o_ref.dtype)

    u = NA_UNROLL
    n_groups = rows // u
    for i in range(u):
        scores(i, i, 0)
    for i in range(u):
        scores(u + i, i, 1)
        softmax_pv(i, i, 0)

    def steady(tt, carry):
        for par in (0, 1):
            r = (2 * tt + par) * u
            for i in range(u):
                scores(r + i, i, par)
                finish(r - 2 * u + i, i)
                softmax_pv(r - u + i, i, 1 - par)
        return carry

    lax.fori_loop(1, n_groups // 2, steady, 0)
    for i in range(u):
        finish(rows - 2 * u + i, i)
        softmax_pv(rows - u + i, i, (n_groups - 1) & 1)
    for i in range(u):
        finish(rows - u + i, i)


def _na_attention(q, k, v, kc, vc, tp):
    b, l, e = q.shape
    ctx = kc.shape[1]
    rows = l // GRID_W
    assert rows >= NA_WIN_H and l % GRID_W == 0
    assert rows % (2 * NA_UNROLL) == 0
    n_pairs = e // LANES
    lat = pl.BlockSpec((1, l, LANES), lambda hp, bi: (bi, 0, hp))
    cspec = pl.BlockSpec((1, ctx, LANES), lambda hp, bi: (bi, 0, hp))
    return pl.pallas_call(
        functools.partial(_na_kernel, rows=rows),
        grid=(n_pairs, b),
        in_specs=[lat, lat, lat, cspec, cspec,
                  pl.BlockSpec((2, 2 * NA_WIN_H - 2, GRID_W, LANES), lambda hp, bi: (hp, 0, 0, 0))],
        out_specs=lat,
        out_shape=jax.ShapeDtypeStruct((b, l, e), BF16),
        scratch_shapes=[
            pltpu.VMEM((2, NA_UNROLL, 2 * GRID_W, NA_WIN_H * GRID_W + ctx), F32),
            pltpu.VMEM((NA_UNROLL, 2 * GRID_W, LANES), F32),
            pltpu.VMEM((NA_UNROLL, 2 * GRID_W, 1), F32),
        ],
        compiler_params=_params("arbitrary", "arbitrary"),
        name="na_attn",
    )(q, k, v, kc, vc, tp)


def _ctx_attn_kernel(q_ref, k_ref, v_ref, o_ref):
    lane_lo = lax.broadcasted_iota(jnp.int32, (1, LANES), 1) < NA_HEAD_DIM
    for hp in range(q_ref.shape[-1] // LANES):
        cols = slice(hp * LANES, (hp + 1) * LANES)
        q = q_ref[0, :, cols] * (NA_HEAD_DIM ** -0.5)
        zero = jnp.zeros_like(q)
        qq = jnp.concatenate([jnp.where(lane_lo, q, zero), jnp.where(lane_lo, zero, q)], axis=0)
        s = _dot_nt(qq, k_ref[0, :, cols])
        m = jnp.max(s, axis=-1, keepdims=True)
        p = jnp.exp(s - m)
        o2 = _dot(p.astype(BF16), v_ref[0, :, cols]) / jnp.sum(p, axis=-1, keepdims=True)
        o_ref[0, :, cols] = _head_pair_select(o2, lane_lo).astype(o_ref.dtype)


def _ctx_attention(q, k, v):
    b, ctx, e = q.shape
    spec = pl.BlockSpec((1, ctx, e), lambda bi: (bi, 0, 0))
    return pl.pallas_call(
        _ctx_attn_kernel,
        grid=(b,),
        in_specs=[spec, spec, spec],
        out_specs=spec,
        out_shape=jax.ShapeDtypeStruct((b, ctx, e), BF16),
        compiler_params=_params("arbitrary"),
        name="ctx_attn",
    )(q, k, v)


def _gated_residual(x_ref, o_ref, z_ref, gate_ref, w_ref):
    z = z_ref[0].astype(F32)
    u = (o_ref[0].astype(F32) * (z * jax.nn.sigmoid(z))).astype(BF16)
    return x_ref[0] + gate_ref[0] * _dot(u, w_ref[...])


def _out_proj_kernel(x_ref, o_ref, z_ref, gate_ref, w_ref, fnw_ref, out_ref, *, final_norm):
    xn = _gated_residual(x_ref, o_ref, z_ref, gate_ref, w_ref)
    if final_norm:
        ms = jnp.mean(xn * xn, axis=-1, keepdims=True)
        xn = xn * lax.rsqrt(ms + EPS) * fnw_ref[...]
    out_ref[0] = xn


def _out_then_proj_kernel(x_ref, o_ref, z_ref, gate_ref, wo_ref, shift_ref, scale_ref, nw_ref, wi_ref,
                          x_out_ref, *out_refs):
    xn = _gated_residual(x_ref, o_ref, z_ref, gate_ref, wo_ref)
    x_out_ref[0] = xn
    _project(_modulated_norm(xn, nw_ref, shift_ref, scale_ref), wi_ref, out_refs)


def _out_then_proj(x, o, z, gate, w_out_bf16, shift, scale, norm_w, w_in_bf16, out_dtypes, name):
    bx, t, d = x.shape
    n = w_in_bf16.shape[1]
    tt = min(256, t)
    row = lambda width: pl.BlockSpec((1, tt, width), lambda b, i: (b, i, 0))
    mod_spec = pl.BlockSpec((1, 1, d), lambda b, i: (b, 0, 0))
    return pl.pallas_call(
        _out_then_proj_kernel,
        grid=(bx, t // tt),
        in_specs=[
            row(d), row(BRANCH), row(BRANCH), mod_spec,
            pl.BlockSpec((BRANCH, d), lambda b, i: (0, 0)),
            mod_spec, mod_spec,
            pl.BlockSpec((1, d), lambda b, i: (0, 0)),
            pl.BlockSpec((d, n), lambda b, i: (0, 0)),
        ],
        out_specs=[row(d)] + [row(BRANCH) for _ in out_dtypes],
        out_shape=[jax.ShapeDtypeStruct((bx, t, d), F32)]
        + [jax.ShapeDtypeStruct((bx, t, BRANCH), dt) for dt in out_dtypes],
        compiler_params=_params("arbitrary", "arbitrary"),
        name=name,
    )(x, o, z, gate, w_out_bf16, shift, scale, norm_w.reshape(1, d), w_in_bf16)


def _out_proj(x, o, z, gate, w_bf16, final_norm_w, final_norm, name):
    bx, t, d = x.shape
    tt = min(512, t)
    row = lambda width: pl.BlockSpec((1, tt, width), lambda b, i: (b, i, 0))
    return pl.pallas_call(
        functools.partial(_out_proj_kernel, final_norm=final_norm),
        grid=(bx, t // tt),
        in_specs=[
            row(d), row(BRANCH), row(BRANCH),
            pl.BlockSpec((1, 1, d), lambda b, i: (b, 0, 0)),
            pl.BlockSpec((BRANCH, d), lambda b, i: (0, 0)),
            pl.BlockSpec((1, d), lambda b, i: (0, 0)),
        ],
        out_specs=row(d),
        out_shape=jax.ShapeDtypeStruct((bx, t, d), F32),
        compiler_params=_params("arbitrary", "arbitrary"),
        name=name,
    )(x, o, z, gate, w_bf16, final_norm_w.reshape(1, d))


def _hgrn_kernel(q_ref, v_ref, ff_ref, fb_ref, vc_ref, ffc_ref, fbc_ref, lower_ref, hnw_ref, o_ref,
                 st_ref, o_dir_ref, dmask_ref, lmask_ref, sgn_ref, g_ref, p_ref, qk_ref, dl_ref, *, n_lat, n_ctx, layer):
    c = HG_CHUNK
    dk = HG_HEAD_DIM

    lw = lower_ref[...]
    e = jnp.exp(lw - jnp.max(lw, axis=0, keepdims=True))
    sm = e / jnp.sum(e, axis=0, keepdims=True)
    lb = jnp.zeros((1, dk), F32)
    for j in range(1, layer + 1):
        lb = lb + sm[j:j + 1]
    one_m_lb = 1.0 - lb

    row = lax.broadcasted_iota(jnp.int32, (c, c), 0)
    col = lax.broadcasted_iota(jnp.int32, (c, c), 1)
    rowl = lax.broadcasted_iota(jnp.int32, (c, dk), 0)
    tri = {False: jnp.tile((col <= row).astype(BF16), (1, 2)), True: jnp.tile((col >= row).astype(BF16), (1, 2))}

    for rev in (False, True):
        for dl in range(HG_DIAG):
            in_blk = (row & (HG_DIAG - 1)) + dl < HG_DIAG if rev else (row & (HG_DIAG - 1)) >= dl
            hit = col == (row + dl if rev else row - dl)
            dmask_ref[int(rev), dl] = (hit & in_blk).astype(F32)
        for li, half in enumerate(HG_LEVELS):
            shift = int(np.log2(2 * half))
            same = (row >> shift) == (col >> shift)
            r2 = (row & (2 * half - 1)) >= half
            c2 = (col & (2 * half - 1)) >= half
            pair = (jnp.logical_not(r2) & c2) if rev else (r2 & jnp.logical_not(c2))
            lmask_ref[int(rev), li] = (same & pair).astype(F32)
            second = (rowl & (2 * half - 1)) >= half
            sgn_ref[int(rev), li] = jnp.where(second, -1.0, 1.0) if rev else jnp.where(second, 1.0, -1.0)

    def gates(raw):
        f = lb + one_m_lb * jax.nn.sigmoid(raw)
        return f, 1.0 - f, jnp.log2(jnp.maximum(f, F32_TINY))

    def decay_sums(raw, rev):
        f, kk, logf = gates(raw)
        a = _dot(tri[rev], jnp.concatenate(_split2(logf), axis=0))
        return f, kk, a

    ctx_chunks = [(d, pl.ds((n_ctx - 1 - j if d else j) * c, c)) for d in (0, 1) for j in range(n_ctx)]
    ctx_sums = [decay_sums((fbc_ref if d else ffc_ref)[0, sl, :], bool(d)) for d, sl in ctx_chunks]

    def ctx_states():
        terms = []
        for (d, sl), (_, kk, a) in zip(ctx_chunks, ctx_sums):
            last = a[0:1] if d else a[c - 1:c]
            terms.append((jnp.exp2(last), _dot_tn(vc_ref[0, sl, :], (kk * jnp.exp2(last - a)).astype(BF16))))
        for d in (0, 1):
            st = jnp.zeros((dk, dk), F32)
            for dlast, upd in terms[d * n_ctx:(d + 1) * n_ctx]:
                st = st * dlast + upd
            st_ref[d] = st

    def state_update(d, vb, kst, dlast):
        st = st_ref[d]
        st_ref[d] = st * dlast + _dot_tn(vb, kst)
        return st

    qscale = HG_HEAD_DIM ** -0.5

    def rows_of(d, step):
        cidx = (n_lat - 1 - step) if d else step
        if isinstance(cidx, int):
            return pl.ds(cidx * c, c)
        return pl.ds(pl.multiple_of(cidx * c, c), c)

    def stage_sums(d, step, slot):
        raw_ref = fb_ref if d else ff_ref
        f, kk, a = decay_sums(raw_ref[0, rows_of(d, step), :], bool(d))
        g_ref[d, slot, 0] = f
        g_ref[d, slot, 1] = kk
        g_ref[d, slot, 2] = a

    def stage_scores(d, step, slot):
        rev = bool(d)
        f = g_ref[d, slot, 0]
        kk = g_ref[d, slot, 1]
        a = g_ref[d, slot, 2]
        qr = q_ref[0, rows_of(d, step), :].astype(F32)
        qs = qr * jax.nn.sigmoid(qr) * qscale
        last = a[0:1] if rev else a[c - 1:c]
        qs_b = qs.astype(BF16)
        kk_b = kk.astype(BF16)
        p = None
        for li, half in enumerate(HG_LEVELS):
            pieces = []
            for blk in range(c // (2 * half)):
                rr = blk * 2 * half + (half - 1 if rev else half)
                pieces.append(jnp.broadcast_to(a[rr:rr + 1], (2 * half, dk)))
            aref = pieces[0] if len(pieces) == 1 else jnp.concatenate(pieces, axis=0)
            x = jnp.exp2((a - aref) * sgn_ref[d, li]).astype(BF16)
            pl_ = _dot_nt(qs_b * x, kk_b * x) * lmask_ref[d, li]
            p = pl_ if p is None else p + pl_
        f3 = f.reshape(c // SUBLANES, SUBLANES, dk)
        g3 = kk.reshape(c // SUBLANES, SUBLANES, dk)
        for dl in range(HG_DIAG):
            if dl > 0:
                g3 = f3 * pltpu.roll(g3, (SUBLANES - 1) if rev else 1, 1)
            pr = qs * g3.reshape(c, dk)
            p = p + jnp.sum(pr, axis=-1, keepdims=True) * dmask_ref[d, dl]
        p_ref[d, slot] = p.astype(BF16)
        qk_ref[d, slot, 0] = qs_b * jnp.exp2(a).astype(BF16)
        qk_ref[d, slot, 1] = kk_b * jnp.exp2(last - a).astype(BF16)
        dl_ref[d, slot] = jnp.broadcast_to(jnp.exp2(last), (8, dk))

    def stage_output(d, step, slot):
        sl = rows_of(d, step)
        vb = v_ref[0, sl, :]
        pmat = p_ref[d, slot]
        qa = qk_ref[d, slot, 0]
        st = state_update(d, vb, qk_ref[d, slot, 1], dl_ref[d, slot, 0:1, :])
        o_dir_ref[d, sl, :] = _dot_nt(qa, st.astype(BF16)) + _dot(pmat, vb)

    grp = HG_GROUP

    def body(s_out, s_scores, s_sums):
        for i in range(grp):
            for d in (0, 1):
                if s_out is not None:
                    stage_output(d, s_out + i, i)
                if s_scores is not None:
                    stage_scores(d, s_scores + i, i)
                if s_sums is not None:
                    stage_sums(d, s_sums + i, i)

    body(None, None, 0)
    ctx_states()
    body(None, 0, grp)

    def steady(t, carry):
        s0 = t * grp
        body(s0 - 2 * grp, s0 - grp, s0)
        return carry

    lax.fori_loop(2, n_lat // grp, steady, 0)
    body(n_lat - 2 * grp, n_lat - grp, None)
    body(n_lat - grp, None, None)

    o = o_dir_ref[0] + o_dir_ref[1]
    ms = jnp.mean(o * o, axis=-1, keepdims=True)
    o_ref[0] = (o * lax.rsqrt(ms + EPS) * hnw_ref[...]).astype(o_ref.dtype)


def _hgrn_scan(q, v, ff, fb, vc, ffc, fbc, hg_lower, head_norm_w, layer):
    b, l, e = q.shape
    ctx = vc.shape[1]
    depth = hg_lower.shape[0]
    assert l % (HG_CHUNK * HG_GROUP) == 0 and l // (HG_CHUNK * HG_GROUP) >= 2 and ctx % HG_CHUNK == 0
    lat = pl.BlockSpec((1, l, LANES), lambda bi, h: (bi, 0, h))
    cspec = pl.BlockSpec((1, ctx, LANES), lambda bi, h: (bi, 0, h))
    return pl.pallas_call(
        functools.partial(_hgrn_kernel, n_lat=l // HG_CHUNK, n_ctx=ctx // HG_CHUNK, layer=layer),
        grid=(b, e // LANES),
        in_specs=[lat, lat, lat, lat, cspec, cspec, cspec,
                  pl.BlockSpec((depth, LANES), lambda bi, h: (0, h)),
                  pl.BlockSpec((1, LANES), lambda bi, h: (0, 0))],
        out_specs=lat,
        out_shape=jax.ShapeDtypeStruct((b, l, e), BF16),
        scratch_shapes=[
            pltpu.VMEM((2, HG_HEAD_DIM, HG_HEAD_DIM), F32),
            pltpu.VMEM((2, l, LANES), F32),
            pltpu.VMEM((2, HG_DIAG, HG_CHUNK, HG_CHUNK), F32),
            pltpu.VMEM((2, len(HG_LEVELS), HG_CHUNK, HG_CHUNK), F32),
            pltpu.VMEM((2, len(HG_LEVELS), HG_CHUNK, LANES), F32),
            pltpu.VMEM((2, HG_GROUP, 3, HG_CHUNK, LANES), F32),
            pltpu.VMEM((2, HG_GROUP, HG_CHUNK, HG_CHUNK), BF16),
            pltpu.VMEM((2, HG_GROUP, 2, HG_CHUNK, LANES), BF16),
            pltpu.VMEM((2, HG_GROUP, 8, LANES), F32),
        ],
        compiler_params=_params("arbitrary", "arbitrary"),
        name="hgrn_scan",
    )(q, v, ff, fb, vc, ffc, fbc, hg_lower, head_norm_w.reshape(1, LANES))


def kernel(x, c, ctx, c_ctx, ada_w, ada_b, norm_w, na_w_in, na_rpb, na_w_out, hg_w_in, hg_lower, hg_norm_w,
           hg_w_out, final_norm_w):
    b, l, d = x.shape
    assert d == D_MODEL and HG_HEAD_DIM == LANES and 2 * NA_HEAD_DIM == LANES

    pad = (-(b + 1)) % 8
    cond = jnp.concatenate([c, c_ctx[None, :], jnp.zeros((pad, d), F32)], axis=0)
    mod = _adaln(cond, ada_w, ada_b)

    def mods(i):
        m = mod[i]
        lat = [m[:b, j * d:(j + 1) * d].reshape(b, 1, d) for j in range(3)]
        cx = [jnp.broadcast_to(m[b:b + 1, j * d:(j + 1) * d].reshape(1, 1, d), (b, 1, d)) for j in range(3)]
        return lat, cx

    (shift, scale, gate), (shift_c, scale_c, gate_c) = mods(0)
    w_in = na_w_in[0].astype(BF16)
    w_out = na_w_out[0].astype(BF16)
    q, k, v, z = _proj(x, shift, scale, norm_w[0], w_in, [BF16] * 4, "na_proj")
    qc, kc, vc, zc = _proj(ctx, shift_c, scale_c, norm_w[0], w_in, [BF16] * 4, "na_proj_ctx")
    o = _na_attention(q, k, v, kc, vc, _na_bias_table(na_rpb[0]))
    oc = _ctx_attention(qc, kc, vc)
    (shift1, scale1, gate1), (shift1_c, scale1_c, _) = mods(1)
    w_in1 = hg_w_in[0].astype(BF16)
    dts = [BF16, BF16, F32, F32, BF16]
    x1, q, i_lat, f_fwd, f_bwd, g = _out_then_proj(
        x, o, z, gate, w_out, shift1, scale1, norm_w[1], w_in1, dts, "na_out_hg_proj")
    _, _, i_ctx, fc_fwd, fc_bwd, _ = _out_then_proj(
        ctx, oc, zc, gate_c, w_out, shift1_c, scale1_c, norm_w[1], w_in1, dts, "na_out_hg_proj_ctx")

    o = _hgrn_scan(q, i_lat, f_fwd, f_bwd, i_ctx, fc_fwd, fc_bwd, hg_lower, hg_norm_w[0], 1)
    return _out_proj(x1, o, g, gate1, hg_w_out[0].astype(BF16), final_norm_w, True, "hg_out")
```

```python
import functools

import numpy as np
import jax
import jax.numpy as jnp
from jax import lax
from jax.experimental import pallas as pl
from jax.experimental.pallas import tpu as pltpu

D_MODEL = 1024
BRANCH = D_MODEL
GRID_W = 64
NA_HEADS = 16
NA_HEAD_DIM = BRANCH // NA_HEADS
NA_WIN_H = 8
NA_WIN_W = 16
HG_HEADS = 8
HG_HEAD_DIM = BRANCH // HG_HEADS
HG_CHUNK = 64
HG_DIAG = 4
HG_LEVELS = (32, 16, 8, 4)
HG_GROUP = 8
NA_UNROLL = 4
EPS = 1e-6
LANES = 128
SUBLANES = 8
F32_TINY = 1e-37
VMEM_LIMIT = 56 * 1024 * 1024
PROJ_TN = 512

BF16 = jnp.bfloat16
F32 = jnp.float32
NT_DIMS = (((1,), (1,)), ((), ()))
TN_DIMS = (((0,), (0,)), ((), ()))


def _dot(a, b):
    return jnp.dot(a, b, preferred_element_type=F32)


def _dot_nt(a, b):
    return lax.dot_general(a, b, NT_DIMS, preferred_element_type=F32)


def _dot_tn(a, b):
    return lax.dot_general(a, b, TN_DIMS, preferred_element_type=F32)


def _split2(x):
    hi = x.astype(BF16)
    return hi, (x - hi.astype(F32)).astype(BF16)


def _params(*sem):
    return pltpu.CompilerParams(dimension_semantics=sem, vmem_limit_bytes=VMEM_LIMIT)


def _adaln_kernel(cond_ref, w_ref, b_ref, out_ref):
    c = cond_ref[...]
    a = c * jax.nn.sigmoid(c)
    w = w_ref[0]
    a_hi = a.astype(BF16)
    a_lo = (a - a_hi.astype(F32)).astype(BF16)
    w_hi = w.astype(BF16)
    w_lo = (w - w_hi.astype(F32)).astype(BF16)
    acc = _dot(a_hi, w_hi) + _dot(a_hi, w_lo) + _dot(a_lo, w_hi)
    out_ref[0] = acc + b_ref[0]


def _adaln(cond, ada_w, ada_b):
    depth, d, n = ada_w.shape
    rows = cond.shape[0]
    tn = 512
    return pl.pallas_call(
        _adaln_kernel,
        grid=(depth, n // tn),
        in_specs=[
            pl.BlockSpec((rows, d), lambda l, j: (0, 0)),
            pl.BlockSpec((1, d, tn), lambda l, j: (l, 0, j)),
            pl.BlockSpec((1, 1, tn), lambda l, j: (l, 0, j)),
        ],
        out_specs=pl.BlockSpec((1, rows, tn), lambda l, j: (l, 0, j)),
        out_shape=jax.ShapeDtypeStruct((depth, rows, n), F32),
        compiler_params=_params("arbitrary", "arbitrary"),
        name="adaln",
    )(cond, ada_w, ada_b.reshape(depth, 1, n))


def _modulated_norm(x, nw_ref, shift_ref, scale_ref):
    ms = jnp.mean(x * x, axis=-1, keepdims=True)
    y = x * lax.rsqrt(ms + EPS) * nw_ref[...]
    return (y * (1.0 + scale_ref[0]) + shift_ref[0]).astype(BF16)


def _project(h, w_ref, out_refs):
    for g, o_ref in enumerate(out_refs):
        for j in range(BRANCH // PROJ_TN):
            c0 = g * BRANCH + j * PROJ_TN
            o_ref[0, :, j * PROJ_TN:(j + 1) * PROJ_TN] = _dot(h, w_ref[:, c0:c0 + PROJ_TN]).astype(o_ref.dtype)


def _proj_kernel(x_ref, shift_ref, scale_ref, nw_ref, w_ref, *out_refs):
    _project(_modulated_norm(x_ref[0], nw_ref, shift_ref, scale_ref), w_ref, out_refs)


def _proj(x, shift, scale, norm_w, w_bf16, out_dtypes, name):
    bx, t, d = x.shape
    n = w_bf16.shape[1]
    tt = min(512, t)
    row_spec = pl.BlockSpec((1, tt, d), lambda b, i: (b, i, 0))
    mod_spec = pl.BlockSpec((1, 1, d), lambda b, i: (b, 0, 0))
    return pl.pallas_call(
        _proj_kernel,
        grid=(bx, t // tt),
        in_specs=[
            row_spec, mod_spec, mod_spec,
            pl.BlockSpec((1, d), lambda b, i: (0, 0)),
            pl.BlockSpec((d, n), lambda b, i: (0, 0)),
        ],
        out_specs=[pl.BlockSpec((1, tt, BRANCH), lambda b, i: (b, i, 0)) for _ in out_dtypes],
        out_shape=[jax.ShapeDtypeStruct((bx, t, BRANCH), dt) for dt in out_dtypes],
        compiler_params=_params("arbitrary", "arbitrary"),
        name=name,
    )(x, shift, scale, norm_w.reshape(1, d), w_bf16)


def _na_bias_table(rpb):
    cols = np.arange(GRID_W)
    col_start = np.clip(cols - NA_WIN_W // 2, 0, GRID_W - NA_WIN_W)
    kc = cols[None, :]
    qc = cols[:, None]
    in_window = (kc >= col_start[:, None]) & (kc < col_start[:, None] + NA_WIN_W)
    dc = np.clip(kc - qc + NA_WIN_W - 1, 0, 2 * NA_WIN_W - 2)
    t15 = jnp.where(jnp.asarray(in_window)[None, None], rpb[:, :, jnp.asarray(dc)], -jnp.inf)
    return jnp.concatenate([t15[:, :-1], t15[:, 1:]], axis=-1).astype(F32)


def _head_pair_select(o2, lane_lo):
    m = o2.shape[0] // 2
    return jnp.where(lane_lo, o2[:m], o2[m:])


def _na_kernel(q_ref, k_ref, v_ref, kc_ref, vc_ref, tp_ref, o_ref, s_ref, o2_ref, inv_ref, *, rows):
    lane_lo = lax.broadcasted_iota(jnp.int32, (1, LANES), 1) < NA_HEAD_DIM
    kc = kc_ref[0]
    vc = vc_ref[0]
    kh = NA_WIN_H
    n_loc = kh * GRID_W
    scale = NA_HEAD_DIM ** -0.5

    def window(r):
        if isinstance(r, int):
            r0 = min(max(r - kh // 2, 0), rows - kh)
            return pl.ds(r0 * GRID_W, n_loc), r0 - r + (NA_WIN_H - 1)
        r0 = jnp.clip(r - kh // 2, 0, rows - kh)
        return pl.ds(pl.multiple_of(r0 * GRID_W, GRID_W), n_loc), r0 - r + (NA_WIN_H - 1)

    def query_rows(r):
        if isinstance(r, int):
            return pl.ds(r * GRID_W, GRID_W)
        return pl.ds(pl.multiple_of(r * GRID_W, GRID_W), GRID_W)

    def scores(r, slot, par):
        q = q_ref[0, query_rows(r), :] * scale
        zero = jnp.zeros_like(q)
        qq = jnp.concatenate([jnp.where(lane_lo, q, zero), jnp.where(lane_lo, zero, q)], axis=0)
        keys, _ = window(r)
        s_ref[par, slot, :, :n_loc] = _dot_nt(qq, k_ref[0, keys, :])
        s_ref[par, slot, :, n_loc:] = _dot_nt(qq, kc)

    def softmax_pv(r, slot, par):
        keys, dr0 = window(r)
        bias = jnp.concatenate(
            [jnp.concatenate([tp_ref[0, dr0 + 2 * i], tp_ref[1, dr0 + 2 * i]], axis=0) for i in range(kh // 2)],
            axis=1)
        s_loc = s_ref[par, slot, :, :n_loc] + bias
        s_ctx = s_ref[par, slot, :, n_loc:]
        m = jnp.maximum(jnp.max(s_loc, axis=-1, keepdims=True), jnp.max(s_ctx, axis=-1, keepdims=True))
        p_loc = jnp.exp(s_loc - m)
        p_ctx = jnp.exp(s_ctx - m)
        denom = jnp.sum(p_loc, axis=-1, keepdims=True) + jnp.sum(p_ctx, axis=-1, keepdims=True)
        o2_ref[slot] = _dot(p_loc.astype(BF16), v_ref[0, keys, :]) + _dot(p_ctx.astype(BF16), vc)
        inv_ref[slot] = 1.0 / denom

    def finish(r, slot):
        o2 = o2_ref[slot] * inv_ref[slot]
        o_ref[0, query_rows(r), :] = _head_pair_select(o2, lane_lo).astype(o_ref.dtype)

    u = NA_UNROLL
    n_groups = rows // u
    for i in range(u):
        scores(i, i, 0)
    for i in range(u):
        scores(u + i, i, 1)
        softmax_pv(i, i, 0)

    def steady(tt, carry):
        for par in (0, 1):
            r = (2 * tt + par) * u
            for i in range(u):
                scores(r + i, i, par)
                finish(r - 2 * u + i, i)
                softmax_pv(r - u + i, i, 1 - par)
        return carry

    lax.fori_loop(1, n_groups // 2, steady, 0)
    for i in range(u):
        finish(rows - 2 * u + i, i)
        softmax_pv(rows - u + i, i, (n_groups - 1) & 1)
    for i in range(u):
        finish(rows - u + i, i)


def _na_attention(q, k, v, kc, vc, tp):
    b, l, e = q.shape
    ctx = kc.shape[1]
    rows = l // GRID_W
    assert rows >= NA_WIN_H and l % GRID_W == 0
    assert rows % (2 * NA_UNROLL) == 0
    n_pairs = e // LANES
    lat = pl.BlockSpec((1, l, LANES), lambda hp, bi: (bi, 0, hp))
    cspec = pl.BlockSpec((1, ctx, LANES), lambda hp, bi: (bi, 0, hp))
    return pl.pallas_call(
        functools.partial(_na_kernel, rows=rows),
        grid=(n_pairs, b),
        in_specs=[lat, lat, lat, cspec, cspec,
                  pl.BlockSpec((2, 2 * NA_WIN_H - 2, GRID_W, LANES), lambda hp, bi: (hp, 0, 0, 0))],
        out_specs=lat,
        out_shape=jax.ShapeDtypeStruct((b, l, e), BF16),
        scratch_shapes=[
            pltpu.VMEM((2, NA_UNROLL, 2 * GRID_W, NA_WIN_H * GRID_W + ctx), F32),
            pltpu.VMEM((NA_UNROLL, 2 * GRID_W, LANES), F32),
            pltpu.VMEM((NA_UNROLL, 2 * GRID_W, 1), F32),
        ],
        compiler_params=_params("arbitrary", "arbitrary"),
        name="na_attn",
    )(q, k, v, kc, vc, tp)


def _ctx_attn_kernel(q_ref, k_ref, v_ref, o_ref):
    lane_lo = lax.broadcasted_iota(jnp.int32, (1, LANES), 1) < NA_HEAD_DIM
    for hp in range(q_ref.shape[-1] // LANES):
        cols = slice(hp * LANES, (hp + 1) * LANES)
        q = q_ref[0, :, cols] * (NA_HEAD_DIM ** -0.5)
        zero = jnp.zeros_like(q)
        qq = jnp.concatenate([jnp.where(lane_lo, q, zero), jnp.where(lane_lo, zero, q)], axis=0)
        s = _dot_nt(qq, k_ref[0, :, cols])
        m = jnp.max(s, axis=-1, keepdims=True)
        p = jnp.exp(s - m)
        o2 = _dot(p.astype(BF16), v_ref[0, :, cols]) / jnp.sum(p, axis=-1, keepdims=True)
        o_ref[0, :, cols] = _head_pair_select(o2, lane_lo).astype(o_ref.dtype)


def _ctx_attention(q, k, v):
    b, ctx, e = q.shape
    spec = pl.BlockSpec((1, ctx, e), lambda bi: (bi, 0, 0))
    return pl.pallas_call(
        _ctx_attn_kernel,
        grid=(b,),
        in_specs=[spec, spec, spec],
        out_specs=spec,
        out_shape=jax.ShapeDtypeStruct((b, ctx, e), BF16),
        compiler_params=_params("arbitrary"),
        name="ctx_attn",
    )(q, k, v)


def _gated_residual(x_ref, o_ref, z_ref, gate_ref, w_ref):
    z = z_ref[0].astype(F32)
    u = (o_ref[0].astype(F32) * (z * jax.nn.sigmoid(z))).astype(BF16)
    return x_ref[0] + gate_ref[0] * _dot(u, w_ref[...])


def _out_proj_kernel(x_ref, o_ref, z_ref, gate_ref, w_ref, fnw_ref, out_ref, *, final_norm):
    xn = _gated_residual(x_ref, o_ref, z_ref, gate_ref, w_ref)
    if final_norm:
        ms = jnp.mean(xn * xn, axis=-1, keepdims=True)
        xn = xn * lax.rsqrt(ms + EPS) * fnw_ref[...]
    out_ref[0] = xn


def _out_then_proj_kernel(x_ref, o_ref, z_ref, gate_ref, wo_ref, shift_ref, scale_ref, nw_ref, wi_ref,
                          x_out_ref, *out_refs):
    xn = _gated_residual(x_ref, o_ref, z_ref, gate_ref, wo_ref)
    x_out_ref[0] = xn
    _project(_modulated_norm(xn, nw_ref, shift_ref, scale_ref), wi_ref, out_refs)


def _out_then_proj(x, o, z, gate, w_out_bf16, shift, scale, norm_w, w_in_bf16, out_dtypes, name):
    bx, t, d = x.shape
    n = w_in_bf16.shape[1]
    tt = min(512, t)
    row = lambda width: pl.BlockSpec((1, tt, width), lambda b, i: (b, i, 0))
    mod_spec = pl.BlockSpec((1, 1, d), lambda b, i: (b, 0, 0))
    once = pl.Buffered(buffer_count=1)
    return pl.pallas_call(
        _out_then_proj_kernel,
        grid=(bx, t // tt),
        in_specs=[
            row(d), row(BRANCH), row(BRANCH), mod_spec,
            pl.BlockSpec((BRANCH, d), lambda b, i: (0, 0), pipeline_mode=once),
            mod_spec, mod_spec,
            pl.BlockSpec((1, d), lambda b, i: (0, 0)),
            pl.BlockSpec((d, n), lambda b, i: (0, 0), pipeline_mode=once),
        ],
        out_specs=[row(d)] + [row(BRANCH) for _ in out_dtypes],
        out_shape=[jax.ShapeDtypeStruct((bx, t, d), F32)]
        + [jax.ShapeDtypeStruct((bx, t, BRANCH), dt) for dt in out_dtypes],
        compiler_params=_params("arbitrary", "arbitrary"),
        name=name,
    )(x, o, z, gate, w_out_bf16, shift, scale, norm_w.reshape(1, d), w_in_bf16)


def _out_proj(x, o, z, gate, w_bf16, final_norm_w, final_norm, name):
    bx, t, d = x.shape
    tt = min(512, t)
    row = lambda width: pl.BlockSpec((1, tt, width), lambda b, i: (b, i, 0))
    return pl.pallas_call(
        functools.partial(_out_proj_kernel, final_norm=final_norm),
        grid=(bx, t // tt),
        in_specs=[
            row(d), row(BRANCH), row(BRANCH),
            pl.BlockSpec((1, 1, d), lambda b, i: (b, 0, 0)),
            pl.BlockSpec((BRANCH, d), lambda b, i: (0, 0)),
            pl.BlockSpec((1, d), lambda b, i: (0, 0)),
        ],
        out_specs=row(d),
        out_shape=jax.ShapeDtypeStruct((bx, t, d), F32),
        compiler_params=_params("arbitrary", "arbitrary"),
        name=name,
    )(x, o, z, gate, w_bf16, final_norm_w.reshape(1, d))


def _hgrn_kernel(q_ref, v_ref, ff_ref, fb_ref, vc_ref, ffc_ref, fbc_ref, lower_ref, hnw_ref, o_ref,
                 st_ref, o_dir_ref, dmask_ref, lmask_ref, sgn_ref, g_ref, p_ref, qk_ref, dl_ref, *, n_lat, n_ctx, layer):
    c = HG_CHUNK
    dk = HG_HEAD_DIM

    lw = lower_ref[...]
    e = jnp.exp(lw - jnp.max(lw, axis=0, keepdims=True))
    sm = e / jnp.sum(e, axis=0, keepdims=True)
    lb = jnp.zeros((1, dk), F32)
    for j in range(1, layer + 1):
        lb = lb + sm[j:j + 1]
    one_m_lb = 1.0 - lb

    row = lax.broadcasted_iota(jnp.int32, (c, c), 0)
    col = lax.broadcasted_iota(jnp.int32, (c, c), 1)
    rowl = lax.broadcasted_iota(jnp.int32, (c, dk), 0)
    tri = {False: jnp.tile((col <= row).astype(BF16), (1, 2)), True: jnp.tile((col >= row).astype(BF16), (1, 2))}

    for rev in (False, True):
        for dl in range(HG_DIAG):
            in_blk = (row & (HG_DIAG - 1)) + dl < HG_DIAG if rev else (row & (HG_DIAG - 1)) >= dl
            hit = col == (row + dl if rev else row - dl)
            dmask_ref[int(rev), dl] = (hit & in_blk).astype(F32)
        for li, half in enumerate(HG_LEVELS):
            shift = int(np.log2(2 * half))
            same = (row >> shift) == (col >> shift)
            r2 = (row & (2 * half - 1)) >= half
            c2 = (col & (2 * half - 1)) >= half
            pair = (jnp.logical_not(r2) & c2) if rev else (r2 & jnp.logical_not(c2))
            lmask_ref[int(rev), li] = (same & pair).astype(F32)
            second = (rowl & (2 * half - 1)) >= half
            sgn_ref[int(rev), li] = jnp.where(second, -1.0, 1.0) if rev else jnp.where(second, 1.0, -1.0)

    def gates(raw):
        f = lb + one_m_lb * jax.nn.sigmoid(raw)
        return f, 1.0 - f, jnp.log2(jnp.maximum(f, F32_TINY))

    def decay_sums(raw, rev):
        f, kk, logf = gates(raw)
        a = _dot(tri[rev], jnp.concatenate(_split2(logf), axis=0))
        return f, kk, a

    ctx_chunks = [(d, pl.ds((n_ctx - 1 - j if d else j) * c, c)) for d in (0, 1) for j in range(n_ctx)]
    ctx_sums = [decay_sums((fbc_ref if d else ffc_ref)[0, sl, :], bool(d)) for d, sl in ctx_chunks]

    def ctx_states():
        terms = []
        for (d, sl), (_, kk, a) in zip(ctx_chunks, ctx_sums):
            last = a[0:1] if d else a[c - 1:c]
            terms.append((jnp.exp2(last), _dot_tn(vc_ref[0, sl, :], (kk * jnp.exp2(last - a)).astype(BF16))))
        for d in (0, 1):
            st = jnp.zeros((dk, dk), F32)
            for dlast, upd in terms[d * n_ctx:(d + 1) * n_ctx]:
                st = st * dlast + upd
            st_ref[d] = st

    def state_update(d, vb, kst, dlast):
        st = st_ref[d]
        st_ref[d] = st * dlast + _dot_tn(vb, kst)
        return st

    qscale = HG_HEAD_DIM ** -0.5

    def rows_of(d, step):
        cidx = (n_lat - 1 - step) if d else step
        if isinstance(cidx, int):
            return pl.ds(cidx * c, c)
        return pl.ds(pl.multiple_of(cidx * c, c), c)

    def stage_sums(d, step, slot):
        raw_ref = fb_ref if d else ff_ref
        f, kk, a = decay_sums(raw_ref[0, rows_of(d, step), :], bool(d))
        g_ref[d, slot, 0] = f
        g_ref[d, slot, 1] = kk
        g_ref[d, slot, 2] = a

    def stage_scores(d, step, slot):
        rev = bool(d)
        f = g_ref[d, slot, 0]
        kk = g_ref[d, slot, 1]
        a = g_ref[d, slot, 2]
        qr = q_ref[0, rows_of(d, step), :].astype(F32)
        qs = qr * jax.nn.sigmoid(qr) * qscale
        last = a[0:1] if rev else a[c - 1:c]
        qs_b = qs.astype(BF16)
        kk_b = kk.astype(BF16)
        p = None
        for li, half in enumerate(HG_LEVELS):
            pieces = []
            for blk in range(c // (2 * half)):
                rr = blk * 2 * half + (half - 1 if rev else half)
                pieces.append(jnp.broadcast_to(a[rr:rr + 1], (2 * half, dk)))
            aref = pieces[0] if len(pieces) == 1 else jnp.concatenate(pieces, axis=0)
            x = jnp.exp2((a - aref) * sgn_ref[d, li]).astype(BF16)
            pl_ = _dot_nt(qs_b * x, kk_b * x) * lmask_ref[d, li]
            p = pl_ if p is None else p + pl_
        f3 = f.reshape(c // SUBLANES, SUBLANES, dk)
        g3 = kk.reshape(c // SUBLANES, SUBLANES, dk)
        for dl in range(HG_DIAG):
            if dl > 0:
                g3 = f3 * pltpu.roll(g3, (SUBLANES - 1) if rev else 1, 1)
            pr = qs * g3.reshape(c, dk)
            p = p + jnp.sum(pr, axis=-1, keepdims=True) * dmask_ref[d, dl]
        p_ref[d, slot] = p.astype(BF16)
        qk_ref[d, slot, 0] = qs_b * jnp.exp2(a).astype(BF16)
        qk_ref[d, slot, 1] = kk_b * jnp.exp2(last - a).astype(BF16)
        dl_ref[d, slot] = jnp.broadcast_to(jnp.exp2(last), (8, dk))

    def stage_output(d, step, slot):
        sl = rows_of(d, step)
        vb = v_ref[0, sl, :]
        pmat = p_ref[d, slot]
        qa = qk_ref[d, slot, 0]
        st = state_update(d, vb, qk_ref[d, slot, 1], dl_ref[d, slot, 0:1, :])
        o_dir_ref[d, sl, :] = _dot_nt(qa, st.astype(BF16)) + _dot(pmat, vb)

    grp = HG_GROUP

    def body(s_out, s_scores, s_sums):
        for i in range(grp):
            for d in (0, 1):
                if s_out is not None:
                    stage_output(d, s_out + i, i)
                if s_scores is not None:
                    stage_scores(d, s_scores + i, i)
                if s_sums is not None:
                    stage_sums(d, s_sums + i, i)

    body(None, None, 0)
    ctx_states()
    body(None, 0, grp)

    def steady(t, carry):
        s0 = t * grp
        body(s0 - 2 * grp, s0 - grp, s0)
        return carry

    lax.fori_loop(2, n_lat // grp, steady, 0)
    body(n_lat - 2 * grp, n_lat - grp, None)
    body(n_lat - grp, None, None)

    o = o_dir_ref[0] + o_dir_ref[1]
    ms = jnp.mean(o * o, axis=-1, keepdims=True)
    o_ref[0] = (o * lax.rsqrt(ms + EPS) * hnw_ref[...]).astype(o_ref.dtype)


def _hgrn_scan(q, v, ff, fb, vc, ffc, fbc, hg_lower, head_norm_w, layer):
    b, l, e = q.shape
    ctx = vc.shape[1]
    depth = hg_lower.shape[0]
    assert l % (HG_CHUNK * HG_GROUP) == 0 and l // (HG_CHUNK * HG_GROUP) >= 2 and ctx % HG_CHUNK == 0
    lat = pl.BlockSpec((1, l, LANES), lambda bi, h: (bi, 0, h))
    cspec = pl.BlockSpec((1, ctx, LANES), lambda bi, h: (bi, 0, h))
    return pl.pallas_call(
        functools.partial(_hgrn_kernel, n_lat=l // HG_CHUNK, n_ctx=ctx // HG_CHUNK, layer=layer),
        grid=(b, e // LANES),
        in_specs=[lat, lat, lat, lat, cspec, cspec, cspec,
                  pl.BlockSpec((depth, LANES), lambda bi, h: (0, h)),
                  pl.BlockSpec((1, LANES), lambda bi, h: (0, 0))],
        out_specs=lat,
        out_shape=jax.ShapeDtypeStruct((b, l, e), BF16),
        scratch_shapes=[
            pltpu.VMEM((2, HG_HEAD_DIM, HG_HEAD_DIM), F32),
            pltpu.VMEM((2, l, LANES), F32),
            pltpu.VMEM((2, HG_DIAG, HG_CHUNK, HG_CHUNK), F32),
            pltpu.VMEM((2, len(HG_LEVELS), HG_CHUNK, HG_CHUNK), F32),
            pltpu.VMEM((2, len(HG_LEVELS), HG_CHUNK, LANES), F32),
            pltpu.VMEM((2, HG_GROUP, 3, HG_CHUNK, LANES), F32),
            pltpu.VMEM((2, HG_GROUP, HG_CHUNK, HG_CHUNK), BF16),
            pltpu.VMEM((2, HG_GROUP, 2, HG_CHUNK, LANES), BF16),
            pltpu.VMEM((2, HG_GROUP, 8, LANES), F32),
        ],
        compiler_params=_params("arbitrary", "arbitrary"),
        name="hgrn_scan",
    )(q, v, ff, fb, vc, ffc, fbc, hg_lower, head_norm_w.reshape(1, LANES))


def kernel(x, c, ctx, c_ctx, ada_w, ada_b, norm_w, na_w_in, na_rpb, na_w_out, hg_w_in, hg_lower, hg_norm_w,
           hg_w_out, final_norm_w):
    b, l, d = x.shape
    assert d == D_MODEL and HG_HEAD_DIM == LANES and 2 * NA_HEAD_DIM == LANES

    pad = (-(b + 1)) % 8
    cond = jnp.concatenate([c, c_ctx[None, :], jnp.zeros((pad, d), F32)], axis=0)
    mod = _adaln(cond, ada_w, ada_b)

    def mods(i):
        m = mod[i]
        lat = [m[:b, j * d:(j + 1) * d].reshape(b, 1, d) for j in range(3)]
        cx = [jnp.broadcast_to(m[b:b + 1, j * d:(j + 1) * d].reshape(1, 1, d), (b, 1, d)) for j in range(3)]
        return lat, cx

    (shift, scale, gate), (shift_c, scale_c, gate_c) = mods(0)
    w_in = na_w_in[0].astype(BF16)
    w_out = na_w_out[0].astype(BF16)
    q, k, v, z = _proj(x, shift, scale, norm_w[0], w_in, [BF16] * 4, "na_proj")
    qc, kc, vc, zc = _proj(ctx, shift_c, scale_c, norm_w[0], w_in, [BF16] * 4, "na_proj_ctx")
    o = _na_attention(q, k, v, kc, vc, _na_bias_table(na_rpb[0]))
    oc = _ctx_attention(qc, kc, vc)
    (shift1, scale1, gate1), (shift1_c, scale1_c, _) = mods(1)
    w_in1 = hg_w_in[0].astype(BF16)
    dts = [BF16, BF16, F32, F32, BF16]
    x1, q, i_lat, f_fwd, f_bwd, g = _out_then_proj(
        x, o, z, gate, w_out, shift1, scale1, norm_w[1], w_in1, dts, "na_out_hg_proj")
    _, _, i_ctx, fc_fwd, fc_bwd, _ = _out_then_proj(
        ctx, oc, zc, gate_c, w_out, shift1_c, scale1_c, norm_w[1], w_in1, dts, "na_out_hg_proj_ctx")

    o = _hgrn_scan(q, i_lat, f_fwd, f_bwd, i_ctx, fc_fwd, fc_bwd, hg_lower, hg_norm_w[0], 1)
    return _out_proj(x1, o, g, gate1, hg_w_out[0].astype(BF16), final_norm_w, True, "hg_out")
```
